```python
import math
import jax
import jax.numpy as jnp
from jax import lax
import numpy as np

D_MODEL = 1024
BATCH = 2
SEQ = 16384
DEPTH = 2
DEC_BATCH = 32
DEC_SEQ = 64
PAST_LEN = 1024

CHUNK = 64
Q_BLOCK = 128
N_EVEN = (DEPTH + 1) // 2
N_ODD = DEPTH // 2
H_A = 4
DK_A = 128
DV_A = 128
CONV_A = 4
H_B = 4
D_B = 128
CONV_C = 31
D_FF = 4 * D_MODEL
MIX_A = H_A * DV_A
MIX_B = H_B * D_B
MIX_W = MIX_A + MIX_B
QKV_A = H_A * (2 * DK_A + DV_A)
SPLITS = (QKV_A, MIX_A, H_A, H_A, MIX_B, MIX_B, MIX_B, H_B)
W_IN_COLS = QKV_A + MIX_A + 2 * H_A + 3 * MIX_B + H_B
FORGET_BIAS_INIT = 3.0
EPS = 1e-6

kernel_name = 'streaming_gdn_fox_conformer_trunk'


def _split_points():
    pts, acc = [], 0
    for n in SPLITS[:-1]:
        acc += n
        pts.append(acc)
    return pts


def rms_norm(x, g):
    xf = x.astype(jnp.float32)
    y = xf * lax.rsqrt(jnp.mean(xf * xf, axis=-1, keepdims=True) + EPS)
    return (y * g.astype(jnp.float32)).astype(x.dtype)


def layer_norm(x, g, b):
    xf = x.astype(jnp.float32)
    xc = xf - jnp.mean(xf, axis=-1, keepdims=True)
    y = xc * lax.rsqrt(jnp.mean(xc * xc, axis=-1, keepdims=True) + EPS)
    return (y * g.astype(jnp.float32) + b.astype(jnp.float32)).astype(x.dtype)


def l2_norm(x):
    xf = x.astype(jnp.float32)
    return xf * lax.rsqrt(jnp.sum(xf * xf, axis=-1, keepdims=True) + EPS)


def causal_depthwise_conv(x, buf, w):
    width, ch = w.shape
    xp = jnp.concatenate([buf.astype(x.dtype), x], axis=1)
    y = lax.conv_general_dilated(xp, w[:, None, :].astype(x.dtype), window_strides=(1,), padding='VALID',
                                 dimension_numbers=('NWC', 'WIO', 'NWC'), feature_group_count=ch)
    return y, xp[:, xp.shape[1] - (width - 1):]


def gdn_block(S, blk):
    q, k, v, lg, beta = blk
    L = q.shape[1]
    qh, kh, vh = (t.transpose(0, 2, 1, 3) for t in (q, k, v))
    gam = jnp.cumsum(lg, axis=1).transpose(0, 2, 1)
    bh = beta.transpose(0, 2, 1)
    t_idx = jnp.arange(L)
    causal = t_idx[:, None] >= t_idx[None, :]
    strict = t_idx[:, None] > t_idx[None, :]
    decay = jnp.exp(jnp.where(causal, gam[..., :, None] - gam[..., None, :], -jnp.inf))
    eg = jnp.exp(gam)
    a_mat = jnp.where(strict, bh[..., :, None] * decay * jnp.einsum('bhtd,bhsd->bhts', kh, kh), 0.0)
    rhs = bh[..., None] * (vh - eg[..., None] * jnp.einsum('bhtd,bhde->bhte', kh, S))
    u = lax.linalg.triangular_solve(a_mat + jnp.eye(L, dtype=a_mat.dtype), rhs,
                                    left_side=True, lower=True, unit_diagonal=True)
    o = (eg[..., None] * jnp.einsum('bhtd,bhde->bhte', qh, S)
         + jnp.einsum('bhts,bhse->bhte', decay * jnp.einsum('bhtd,bhsd->bhts', qh, kh), u))
    g_last = gam[..., -1:]
    S_new = (jnp.exp(g_last)[..., None] * S
             + jnp.einsum('bhs,bhsd,bhse->bhde', jnp.exp(g_last - gam), kh, u))
    return S_new, o.transpose(0, 2, 1, 3)


def gated_delta_rule(S0, q, k, v, lg, beta):
    Bn, L = q.shape[:2]
    if L <= CHUNK:
        return gdn_block(S0, (q, k, v, lg, beta))
    nc = L // CHUNK

    def to_chunks(t):
        return jnp.moveaxis(t.reshape((Bn, nc, CHUNK) + t.shape[2:]), 1, 0)

    S, o = lax.scan(gdn_block, S0, tuple(to_chunks(t) for t in (q, k, v, lg, beta)))
    return S, jnp.moveaxis(o, 0, 1).reshape((Bn, L) + o.shape[3:])


def gated_deltanet(qkv, z, a, b, conv_buf, S0, conv_w, a_log, dt_bias, g_norm):
    Bn, L, _ = qkv.shape
    f32 = jnp.float32
    qkv, new_buf = causal_depthwise_conv(qkv, conv_buf, conv_w)
    qkv = jax.nn.silu(qkv)
    q, k, v = jnp.split(qkv, [H_A * DK_A, 2 * H_A * DK_A], axis=-1)
    q = l2_norm(q.reshape(Bn, L, H_A, DK_A)) * (DK_A ** -0.5)
    k = l2_norm(k.reshape(Bn, L, H_A, DK_A))
    v = v.reshape(Bn, L, H_A, DV_A).astype(f32)
    lg = -jnp.exp(a_log.astype(f32)) * jax.nn.softplus(a.astype(f32) + dt_bias.astype(f32))
    beta = jax.nn.sigmoid(b.astype(f32))
    S, o = gated_delta_rule(S0.astype(f32), q, k, v, lg, beta)
    o = rms_norm(o, g_norm) * jax.nn.silu(z.reshape(Bn, L, H_A, DV_A).astype(f32))
    return o.reshape(Bn, L, MIX_A).astype(z.dtype), new_buf, S


def fox_blocks(q, k, v, c):
    Bn, S = q.shape[:2]
    nb = S // Q_BLOCK
    f32 = jnp.float32
    qf, kf, vf = (t.astype(f32) for t in (q, k, v))
    offs = jnp.arange(Q_BLOCK)

    def query_block(i):
        start = i * Q_BLOCK
        qi = lax.dynamic_slice_in_dim(qf, start, Q_BLOCK, axis=1) * (D_B ** -0.5)
        ci = lax.dynamic_slice_in_dim(c, start, Q_BLOCK, axis=1).transpose(0, 2, 1)
        qpos = start + offs

        def key_block(j, carry):
            m, l, acc = carry
            ks = j * Q_BLOCK
            kj = lax.dynamic_slice_in_dim(kf, ks, Q_BLOCK, axis=1)
            vj = lax.dynamic_slice_in_dim(vf, ks, Q_BLOCK, axis=1)
            cj = lax.dynamic_slice_in_dim(c, ks, Q_BLOCK, axis=1).transpose(0, 2, 1)
            s = jnp.einsum('bqhd,bkhd->bhqk', qi, kj) + ci[..., :, None] - cj[..., None, :]
            s = jnp.where((ks + offs)[None, :] <= qpos[:, None], s, -jnp.inf)
            m_new = jnp.maximum(m, jnp.max(s, axis=-1))
            p = jnp.exp(s - m_new[..., None])
            corr = jnp.exp(m - m_new)
            return (m_new, l * corr + jnp.sum(p, axis=-1),
                    acc * corr[..., None] + jnp.einsum('bhqk,bkhd->bhqd', p, vj))

        init = (jnp.full((Bn, H_B, Q_BLOCK), -jnp.inf, f32), jnp.zeros((Bn, H_B, Q_BLOCK), f32),
                jnp.zeros((Bn, H_B, Q_BLOCK, D_B), f32))
        m, l, acc = lax.fori_loop(0, i + 1, key_block, init)
        return (acc / l[..., None]).transpose(0, 2, 1, 3)

    out = lax.map(query_block, jnp.arange(nb))
    return jnp.moveaxis(out, 0, 1).reshape(Bn, S, H_B, D_B)


def fox_dense(q, k, v, c):
    L, T = q.shape[1], k.shape[1]
    f32 = jnp.float32
    c = c.transpose(0, 2, 1)
    s = (jnp.einsum('bqhd,bkhd->bhqk', q.astype(f32), k.astype(f32)) * (D_B ** -0.5)
         + c[..., T - L:, None] - c[..., None, :])
    qpos = T - L + jnp.arange(L)
    s = jnp.where(jnp.arange(T)[None, :] <= qpos[:, None], s, -jnp.inf)
    p = jax.nn.softmax(s, axis=-1)
    return jnp.einsum('bhqk,bkhd->bqhd', p, v.astype(f32))


def forgetting_attention(q, k, v, f_logit, q_gain, k_gain, past_k, past_v, past_lf):
    Bn, L, _ = q.shape
    q = rms_norm(q.reshape(Bn, L, H_B, D_B), q_gain)
    k = rms_norm(k.reshape(Bn, L, H_B, D_B), k_gain)
    v = v.reshape(Bn, L, H_B, D_B)
    lf = jax.nn.log_sigmoid(f_logit.astype(jnp.float32))
    if past_k is None:
        o = fox_blocks(q, k, v, jnp.cumsum(lf, axis=1))
    else:
        k_all = jnp.concatenate([past_k.astype(k.dtype), k], axis=1)
        v_all = jnp.concatenate([past_v.astype(v.dtype), v], axis=1)
        c_all = jnp.cumsum(jnp.concatenate([past_lf.astype(jnp.float32), lf], axis=1), axis=1)
        o = fox_dense(q, k_all, v_all, c_all)
    return o.reshape(Bn, L, MIX_B).astype(v.dtype), k, v, lf.astype(v.dtype)


def trunk(x, past, p):
    Bn = x.shape[0]
    new = {'conv_a': [], 'delta': [], 'k_b': [], 'v_b': [], 'lf_b': [], 'conv_c': []}
    for layer in range(DEPTH):
        i = layer // 2
        if layer % 2 == 0:
            h = rms_norm(x, p['norm_mix_e'][i])
            qkv_a, z_a, a_a, b_a, q_b, k_b, v_b, f_b = jnp.split(h @ p['w_in_e'][i], _split_points(), axis=-1)
            if past is None:
                buf_a = jnp.zeros((Bn, CONV_A - 1, QKV_A), x.dtype)
                S0 = jnp.zeros((Bn, H_A, DK_A, DV_A), jnp.float32)
                pk = pv = plf = None
            else:
                buf_a, S0 = past['conv_a'][i], past['delta'][i]
                pk, pv, plf = past['k_b'][i], past['v_b'][i], past['lf_b'][i]
            o_a, nbuf_a, S_a = gated_deltanet(qkv_a, z_a, a_a, b_a, buf_a, S0, p['conv_a_w'][i],
                                              p['a_log'][i], p['dt_bias'][i], p['g_norm_a'][i])
            o_b, k_new, v_new, lf_new = forgetting_attention(q_b, k_b, v_b, f_b + p['b_f'][i],
                                                             p['q_norm_b'][i], p['k_norm_b'][i], pk, pv, plf)
            x = x + jnp.concatenate([o_a, o_b.astype(o_a.dtype)], axis=-1) @ p['w_out_e'][i]
            new['conv_a'].append(nbuf_a)
            new['delta'].append(S_a.astype(x.dtype))
            new['k_b'].append(k_new)
            new['v_b'].append(v_new)
            new['lf_b'].append(lf_new)
        else:
            h = rms_norm(x, p['norm_mix_o'][i])
            u = h @ p['w_glu'][i]
            g = u[..., :D_MODEL] * jax.nn.sigmoid(u[..., D_MODEL:])
            buf_c = jnp.zeros((Bn, CONV_C - 1, D_MODEL), x.dtype) if past is None else past['conv_c'][i]
            cv, nbuf_c = causal_depthwise_conv(g, buf_c, p['dw_c'][i])
            cv = jax.nn.silu(layer_norm(cv + p['dw_c_b'][i], p['ln_c_g'][i], p['ln_c_b'][i]))
            x = x + cv @ p['w_pw_c'][i]
            new['conv_c'].append(nbuf_c)
        h = rms_norm(x, p['norm_mlp'][layer])
        x = x + jnp.square(jax.nn.relu(h @ p['w_up'][layer])) @ p['w_down'][layer]
    return x, {name: jnp.stack(rows) for name, rows in new.items()}


def setup_inputs(seed: int = 0) -> dict:
    key = jax.random.key(seed)
    keys = list(jax.random.split(key, 40))
    f32 = jnp.float32

    def nrm(shape, scale=1.0):
        return scale * jax.random.normal(keys.pop(), shape, f32)

    def gain(shape):
        return 1.0 + nrm(shape, 0.02)

    dt = jnp.exp(jax.random.uniform(keys.pop(), (N_EVEN, H_A), f32, math.log(1e-3), math.log(1e-1)))
    return {
        'x_prompt': nrm((BATCH, SEQ, D_MODEL)),
        'x_sample': nrm((DEC_BATCH, DEC_SEQ, D_MODEL)),
        'state_conv_a': nrm((N_EVEN, DEC_BATCH, CONV_A - 1, QKV_A)),
        'state_delta_a': nrm((N_EVEN, DEC_BATCH, H_A, DK_A, DV_A), 0.1),
        'cache_k_b': nrm((N_EVEN, DEC_BATCH, PAST_LEN, H_B, D_B)),
        'cache_v_b': nrm((N_EVEN, DEC_BATCH, PAST_LEN, H_B, D_B)),
        'cache_logf_b': jax.nn.log_sigmoid(FORGET_BIAS_INIT + nrm((N_EVEN, DEC_BATCH, PAST_LEN, H_B))),
        'state_conv_c': nrm((N_ODD, DEC_BATCH, CONV_C - 1, D_MODEL), 0.5),
        'norm_mix_e': gain((N_EVEN, D_MODEL)),
        'w_in_e': nrm((N_EVEN, D_MODEL, W_IN_COLS), D_MODEL ** -0.5),
        'b_f': FORGET_BIAS_INIT + nrm((N_EVEN, H_B), 0.1),
        'conv_a_w': nrm((N_EVEN, CONV_A, QKV_A), CONV_A ** -0.5),
        'a_log': jnp.log(jax.random.uniform(keys.pop(), (N_EVEN, H_A), f32, 1.0, 16.0)),
        'dt_bias': dt + jnp.log(-jnp.expm1(-dt)),
        'g_norm_a': gain((N_EVEN, DV_A)),
        'q_norm_b': gain((N_EVEN, D_B)),
        'k_norm_b': gain((N_EVEN, D_B)),
        'w_out_e': nrm((N_EVEN, MIX_W, D_MODEL), MIX_W ** -0.5),
        'norm_mix_o': gain((N_ODD, D_MODEL)),
        'w_glu': nrm((N_ODD, D_MODEL, 2 * D_MODEL), D_MODEL ** -0.5),
        'dw_c': nrm((N_ODD, CONV_C, D_MODEL), CONV_C ** -0.5),
        'dw_c_b': nrm((N_ODD, D_MODEL), 0.02),
        'ln_c_g': gain((N_ODD, D_MODEL)),
        'ln_c_b': nrm((N_ODD, D_MODEL), 0.02),
        'w_pw_c': nrm((N_ODD, D_MODEL, D_MODEL), D_MODEL ** -0.5),
        'norm_mlp': gain((DEPTH, D_MODEL)),
        'w_up': nrm((DEPTH, D_MODEL, D_FF), D_MODEL ** -0.5),
        'w_down': nrm((DEPTH, D_FF, D_MODEL), D_FF ** -0.5),
    }


def reference(x_prompt, x_sample, state_conv_a, state_delta_a, cache_k_b, cache_v_b, cache_logf_b, state_conv_c,
              norm_mix_e, w_in_e, b_f, conv_a_w, a_log, dt_bias, g_norm_a, q_norm_b, k_norm_b, w_out_e,
              norm_mix_o, w_glu, dw_c, dw_c_b, ln_c_g, ln_c_b, w_pw_c, norm_mlp, w_up, w_down):
    params = {
        'norm_mix_e': norm_mix_e, 'w_in_e': w_in_e, 'b_f': b_f, 'conv_a_w': conv_a_w, 'a_log': a_log,
        'dt_bias': dt_bias, 'g_norm_a': g_norm_a, 'q_norm_b': q_norm_b, 'k_norm_b': k_norm_b,
        'w_out_e': w_out_e, 'norm_mix_o': norm_mix_o, 'w_glu': w_glu, 'dw_c': dw_c, 'dw_c_b': dw_c_b,
        'ln_c_g': ln_c_g, 'ln_c_b': ln_c_b, 'w_pw_c': w_pw_c, 'norm_mlp': norm_mlp, 'w_up': w_up,
        'w_down': w_down,
    }
    y_prompt, ps = trunk(x_prompt, None, params)
    past = {'conv_a': state_conv_a, 'delta': state_delta_a, 'k_b': cache_k_b, 'v_b': cache_v_b,
            'lf_b': cache_logf_b, 'conv_c': state_conv_c}
    y_sample, ss = trunk(x_sample, past, params)
    return (y_prompt, y_sample,
            ps['conv_a'], ps['delta'], ps['k_b'], ps['v_b'], ps['lf_b'], ps['conv_c'],
            ss['conv_a'], ss['delta'], ss['k_b'], ss['v_b'], ss['lf_b'], ss['conv_c'])
```

```python
import functools
import math

import jax
import jax.numpy as jnp
from jax import lax
from jax.experimental import pallas as pl
from jax.experimental.pallas import tpu as pltpu

F32 = jnp.float32
BF16 = jnp.bfloat16
EPS = 1e-6
LANES = 128
VMEM_LIMIT = 56 * 1024 * 1024

H_A = 4
DK_A = 128
DV_A = 128
CONV_A = 4
H_B = 4
D_B = 128
CONV_C = 31
CHUNK = 64
QK_AUG = 2 * D_B
LOG2E = 1.4426950408889634


def _params(*sem):
    return pltpu.CompilerParams(dimension_semantics=sem, vmem_limit_bytes=VMEM_LIMIT)


def _const_spec(shape):
    nd = len(shape)
    return pl.BlockSpec(shape, lambda *_: (0,) * nd, pipeline_mode=pl.Buffered(1))


def _rms(x, gain):
    return x * lax.rsqrt(jnp.mean(x * x, axis=-1, keepdims=True) + EPS) * gain


def _sigmoid(x):
    return 1.0 / (1.0 + jnp.exp(-x))


def _softplus(x):
    return jnp.maximum(x, 0.0) + jnp.log(1.0 + jnp.exp(-jnp.abs(x)))


def _dot(a, b):
    return jnp.dot(a, b, preferred_element_type=F32)


def _dot_nt(a, b):
    return lax.dot_general(a, b, (((1,), (1,)), ((), ())), preferred_element_type=F32)


def _dot_tn(a, b):
    return lax.dot_general(a, b, (((0,), (0,)), ((), ())), preferred_element_type=F32)


def _split2(x):
    hi = x.astype(BF16)
    lo = (x - hi.astype(F32)).astype(BF16)
    return hi, lo


def _split3(x):
    hi = x.astype(BF16)
    r = x - hi.astype(F32)
    mid = r.astype(BF16)
    lo = (r - mid.astype(F32)).astype(BF16)
    return hi, mid, lo


def _dot3(a, b):
    ah, al = _split2(a)
    bh, bl = _split2(b)
    return _dot(ah, bh) + (_dot(ah, bl) + _dot(al, bh))


def _cumsum_rows(x):
    n = x.shape[0]
    row = lax.broadcasted_iota(jnp.int32, x.shape, 0)
    s = 1
    while s < n:
        x = x + jnp.where(row >= s, pltpu.roll(x, s, axis=0), 0.0)
        s *= 2
    return x


def _cumsum_lanes(x):
    n = x.shape[1]
    col = lax.broadcasted_iota(jnp.int32, x.shape, 1)
    s = 1
    while s < n:
        x = x + jnp.where(col >= s, pltpu.roll(x, s, axis=1), 0.0)
        s *= 2
    return x


def _inproj_kernel(x_ref, gain_ref, wm_ref, wg_ref, alog_ref, dtb_ref, bf_ref, qg_ref, kg_ref,
                   qkv_ref, z_ref, qn_ref, kn_ref, v_ref, gates_ref):
    h = _rms(x_ref[...], gain_ref[...]).astype(BF16)
    n_qkv = qkv_ref.shape[1]
    n_z = z_ref.shape[1]
    n_b = qn_ref.shape[1]
    qkv_ref[...] = _dot(h, wm_ref[:, 0:n_qkv])
    z_ref[...] = _dot(h, wm_ref[:, n_qkv:n_qkv + n_z])
    off = n_qkv + n_z
    for hh in range(H_B):
        q = _dot(h, wm_ref[:, off + hh * D_B:off + (hh + 1) * D_B])
        qn_ref[:, hh * D_B:(hh + 1) * D_B] = _rms(q, qg_ref[...])
        k = _dot(h, wm_ref[:, off + n_b + hh * D_B:off + n_b + (hh + 1) * D_B])
        kn_ref[:, hh * D_B:(hh + 1) * D_B] = _rms(k, kg_ref[...])
    v_ref[...] = _dot(h, wm_ref[:, off + 2 * n_b:off + 3 * n_b])
    g = _dot(h, wg_ref[...])
    lane = lax.broadcasted_iota(jnp.int32, g.shape, 1)
    lg = -jnp.exp(alog_ref[...]) * _softplus(g + dtb_ref[...])
    beta = _sigmoid(g)
    lf = -_softplus(-(g + bf_ref[...]))
    gates_ref[...] = jnp.where(lane < H_A, lg,
                               jnp.where(lane < 2 * H_A, beta,
                                         jnp.where(lane < 2 * H_A + H_B, lf, 0.0)))


def _inproj(x, gain, wm, wg, alog, dtb, bfr, qg, kg, tm):
    t, d = x.shape
    n_qkv = H_A * (2 * DK_A + DV_A)
    n_z = H_A * DV_A
    n_b = H_B * D_B
    row = lambda i: (i, 0)
    outs = [n_qkv, n_z, n_b, n_b, n_b, LANES]
    return pl.pallas_call(
        _inproj_kernel,
        grid=(t // tm,),
        in_specs=[pl.BlockSpec((tm, d), row), _const_spec(gain.shape), _const_spec(wm.shape),
                  _const_spec(wg.shape), _const_spec(alog.shape), _const_spec(dtb.shape),
                  _const_spec(bfr.shape), _const_spec(qg.shape), _const_spec(kg.shape)],
        out_specs=[pl.BlockSpec((tm, n), row) for n in outs],
        out_shape=[jax.ShapeDtypeStruct((t, n), F32) for n in outs],
        compiler_params=_params("parallel"),
        name="inproj",
    )(x, gain, wm, wg, alog, dtb, bfr, qg, kg)


def _unit_lower_inverse(a):
    n = a.shape[0]
    r = lax.broadcasted_iota(jnp.int32, (n, n), 0)
    c = lax.broadcasted_iota(jnp.int32, (n, n), 1)
    eye = jnp.where(r == c, 1.0, 0.0).astype(F32)
    same_block = (r // 16) == (c // 16)
    d = jnp.where(same_block, a, 0.0)
    nn = a - d
    d2 = _dot3(d, d)
    d4 = _dot3(d2, d2)
    d8 = _dot3(d4, d4)
    t = eye - d
    t = t + _dot3(t, d2)
    t = t + _dot3(t, d4)
    t = t + _dot3(t, d8)
    m = _dot3(t, nn)
    m2 = _dot3(m, m)
    x = t - _dot3(m, t)
    return x + _dot3(m2, x)


def _gdn_kernel(qkv_ref, z_ref, gates_ref, buf0_ref, s0_ref, cw_ref, gn_ref,
                o_ref, nbuf_ref, sout_ref, xbuf, state):
    c = pl.program_id(1)
    L = CHUNK
    pad = 8

    @pl.when(c == 0)
    def _():
        xbuf[pad - (CONV_A - 1):pad, :] = buf0_ref[0]
        state[...] = s0_ref[0]

    xbuf[pad:pad + L, :] = qkv_ref[0]
    w = cw_ref[...]
    y = xbuf[pad:pad + L, :] * w[CONV_A - 1:CONV_A, :]
    for i in range(1, CONV_A):
        y = y + xbuf[pad - i:pad - i + L, :] * w[CONV_A - 1 - i:CONV_A - i, :]
    tail = xbuf[pad + L - (CONV_A - 1):pad + L, :]
    xbuf[pad - (CONV_A - 1):pad, :] = tail
    nbuf_ref[0] = tail
    y = y * _sigmoid(y)

    gates = gates_ref[0]
    gam_all = _cumsum_rows(gates)
    r = lax.broadcasted_iota(jnp.int32, (L, L), 0)
    cc = lax.broadcasted_iota(jnp.int32, (L, L), 1)
    causal = r >= cc
    strict = r > cc
    ones = jnp.ones((L, L), BF16)
    neg_inf = -jnp.inf

    for h in range(H_A):
        q = y[:, h * DK_A:(h + 1) * DK_A]
        k = y[:, H_A * DK_A + h * DK_A:H_A * DK_A + (h + 1) * DK_A]
        v = y[:, 2 * H_A * DK_A + h * DV_A:2 * H_A * DK_A + (h + 1) * DV_A]
        q = q * lax.rsqrt(jnp.sum(q * q, axis=-1, keepdims=True) + EPS) * (DK_A ** -0.5)
        k = k * lax.rsqrt(jnp.sum(k * k, axis=-1, keepdims=True) + EPS)
        gcol = gam_all[:, h:h + 1]
        beta = gates[:, H_A + h:H_A + h + 1]
        g_last = gam_all[L - 1:L, h:h + 1]
        gb = jnp.broadcast_to(gcol, (L, L))
        diag = jnp.where(r == cc, gb, 0.0)
        dh, dm, dl = _split3(diag)
        grow = _dot(ones, dh) + (_dot(ones, dm) + _dot(ones, dl))
        decay = jnp.exp(jnp.where(causal, gb - grow, neg_inf))
        eg = jnp.exp(gcol)

        kb = k.astype(BF16)
        qb = q.astype(BF16)
        kk = _dot_nt(kb, kb)
        qk = _dot_nt(qb, kb)
        a_mat = jnp.where(strict, beta * decay * kk, 0.0)
        p_mat = decay * qk
        t_inv = _unit_lower_inverse(a_mat)
        u0 = _dot3(t_inv, beta * v)
        wmat = _dot3(t_inv, (beta * eg) * k)

        s_h = state[h]
        s_b = s_h.astype(BF16)
        u = u0 - _dot(wmat.astype(BF16), s_b)
        u_b = u.astype(BF16)
        o = eg * _dot(qb, s_b) + _dot(p_mat.astype(BF16), u_b)
        kd = (jnp.exp(g_last - gcol) * k).astype(BF16)
        state[h] = jnp.exp(g_last) * s_h + _dot_tn(kd, u_b)

        zz = z_ref[0, :, h * DV_A:(h + 1) * DV_A]
        o_ref[0, :, h * DV_A:(h + 1) * DV_A] = _rms(o, gn_ref[...]) * (zz * _sigmoid(zz))

    sout_ref[0] = state[...]


def _gdn(qkv, z, gates, buf0, s0, conv_w, g_norm):
    b, l, n_qkv = qkv.shape
    nc = l // CHUNK
    n_z = z.shape[2]
    blk = lambda bi, ci: (bi, ci, 0)
    per_b3 = lambda bi, ci: (bi, 0, 0)
    per_b4 = lambda bi, ci: (bi, 0, 0, 0)
    return pl.pallas_call(
        _gdn_kernel,
        grid=(b, nc),
        in_specs=[pl.BlockSpec((1, CHUNK, n_qkv), blk), pl.BlockSpec((1, CHUNK, n_z), blk),
                  pl.BlockSpec((1, CHUNK, LANES), blk),
                  pl.BlockSpec((1, CONV_A - 1, n_qkv), per_b3),
                  pl.BlockSpec((1, H_A, DK_A, DV_A), per_b4),
                  _const_spec(conv_w.shape), _const_spec(g_norm.shape)],
        out_specs=[pl.BlockSpec((1, CHUNK, n_z), blk),
                   pl.BlockSpec((1, CONV_A - 1, n_qkv), per_b3),
                   pl.BlockSpec((1, H_A, DK_A, DV_A), per_b4)],
        out_shape=[jax.ShapeDtypeStruct((b, l, n_z), F32),
                   jax.ShapeDtypeStruct((b, CONV_A - 1, n_qkv), F32),
                   jax.ShapeDtypeStruct((b, H_A, DK_A, DV_A), F32)],
        scratch_shapes=[pltpu.VMEM((8 + CHUNK, n_qkv), F32), pltpu.VMEM((H_A, DK_A, DV_A), F32)],
        compiler_params=_params("parallel", "arbitrary"),
        name="gdn",
    )(qkv, z, gates, buf0, s0, conv_w, g_norm)


def _fox_prep_kernel(qn_ref, kn_ref, v_ref, gates_ref, qa_ref, ka_ref, vt_ref, carry):
    i = pl.program_id(1)

    @pl.when(i == 0)
    def _():
        carry[...] = jnp.zeros_like(carry)

    tp = qn_ref.shape[1]
    csum = _cumsum_rows(gates_ref[0]) + carry[...]
    carry[...] = csum[tp - 1:tp, :]
    lane = lax.broadcasted_iota(jnp.int32, (tp, LANES), 1)
    for h in range(H_B):
        c2 = jnp.broadcast_to(csum[:, 2 * H_A + h:2 * H_A + h + 1] * LOG2E, (tp, LANES))
        hi, mid, lo = (piece.astype(F32) for piece in _split3(c2))
        q_aug = jnp.where(lane == 0, hi, jnp.where(lane == 1, mid, jnp.where(lane == 2, lo,
                          jnp.where(lane < 6, 1.0, 0.0)))).astype(BF16)
        k_aug = jnp.where(lane < 3, 1.0, jnp.where(lane == 3, -hi, jnp.where(lane == 4, -mid,
                          jnp.where(lane == 5, -lo, 0.0)))).astype(BF16)
        sl = slice(h * D_B, (h + 1) * D_B)
        qa_ref[0, h, 0, :, 0:D_B] = (qn_ref[0, :, sl] * (D_B ** -0.5 * LOG2E)).astype(BF16)
        qa_ref[0, h, 0, :, D_B:QK_AUG] = q_aug
        ka_ref[0, h, 0, :, 0:D_B] = kn_ref[0, :, sl].astype(BF16)
        ka_ref[0, h, 0, :, D_B:QK_AUG] = k_aug
        vt_ref[0, h, 0] = v_ref[0, :, sl].T.astype(BF16)


def _fox_prep(qn, kn, v, gates, tp):
    b, s, n_b = qn.shape
    nb = s // tp
    blk = lambda bi, i: (bi, i, 0)
    oblk = lambda bi, i: (bi, 0, i, 0, 0)
    return pl.pallas_call(
        _fox_prep_kernel,
        grid=(b, nb),
        in_specs=[pl.BlockSpec((1, tp, n_b), blk), pl.BlockSpec((1, tp, n_b), blk),
                  pl.BlockSpec((1, tp, n_b), blk), pl.BlockSpec((1, tp, LANES), blk)],
        out_specs=[pl.BlockSpec((1, H_B, 1, tp, QK_AUG), oblk),
                   pl.BlockSpec((1, H_B, 1, tp, QK_AUG), oblk),
                   pl.BlockSpec((1, H_B, 1, D_B, tp), oblk)],
        out_shape=[jax.ShapeDtypeStruct((b, H_B, nb, tp, QK_AUG), BF16),
                   jax.ShapeDtypeStruct((b, H_B, nb, tp, QK_AUG), BF16),
                   jax.ShapeDtypeStruct((b, H_B, nb, D_B, tp), BF16)],
        scratch_shapes=[pltpu.VMEM((1, LANES), F32)],
        compiler_params=_params("parallel", "arbitrary"),
        name="fox_prep",
    )(qn, kn, v, gates)


def _fox_kernel(qa_ref, ka_ref, vt_ref, o_ref):
    i = pl.program_id(2)
    q = qa_ref[0, 0, 0]
    tq = q.shape[0]

    def step(j, carry, masked):
        m, l, acc = carry
        kj = ka_ref[0, 0, j]
        vtj = vt_ref[0, 0, j]
        s = _dot_nt(kj, q)
        if masked:
            kpos = lax.broadcasted_iota(jnp.int32, s.shape, 0)
            qpos = lax.broadcasted_iota(jnp.int32, s.shape, 1)
            s = jnp.where(kpos <= qpos, s, -jnp.inf)
        m_new = jnp.maximum(m, jnp.max(s, axis=0, keepdims=True))
        p = jnp.exp2(s - m_new)
        corr = jnp.exp2(m - m_new)
        l = l * corr + jnp.sum(p, axis=0, keepdims=True)
        acc = acc * corr + _dot(vtj, p.astype(BF16))
        return m_new, l, acc

    init = (jnp.full((1, tq), -jnp.inf, F32), jnp.zeros((1, tq), F32), jnp.zeros((D_B, tq), F32))
    carry = lax.fori_loop(0, i, lambda j, cr: step(j, cr, False), init)
    m, l, acc = step(i, carry, True)
    o_ref[0] = (acc / l).T


def _fox(qa, ka, vt):
    b, hb, nb, tq, _ = qa.shape
    return pl.pallas_call(
        _fox_kernel,
        grid=(b, hb, nb),
        in_specs=[pl.BlockSpec((1, 1, 1, tq, QK_AUG), lambda bi, h, i: (bi, h, i, 0, 0)),
                  pl.BlockSpec((1, 1, nb, tq, QK_AUG), lambda bi, h, i: (bi, h, 0, 0, 0)),
                  pl.BlockSpec((1, 1, nb, D_B, tq), lambda bi, h, i: (bi, h, 0, 0, 0))],
        out_specs=pl.BlockSpec((1, tq, D_B), lambda bi, h, i: (bi, i, h)),
        out_shape=jax.ShapeDtypeStruct((b, nb * tq, hb * D_B), F32),
        compiler_params=_params("parallel", "parallel", "arbitrary"),
        name="fox",
    )(qa, ka, vt)


def _fox_dec_kernel(q_ref, kn_ref, vn_ref, kp_ref, vp_ref, lfp_ref, lfn_ref, gates_ref, o_ref):
    h = pl.program_id(1)
    L = q_ref.shape[1]
    q = q_ref[0].astype(BF16)
    t_past = lfp_ref.shape[3]
    c_past = _cumsum_lanes(jnp.broadcast_to(lfp_ref[0, 0], (8, t_past)))[0:1, :]
    total = c_past[:, t_past - 1:t_past]
    c_new_row = total + _cumsum_lanes(jnp.broadcast_to(lfn_ref[0, 0], (8, LANES)))[0:1, :]
    gates = gates_ref[0]
    lane = lax.broadcasted_iota(jnp.int32, gates.shape, 1)
    c_col_all = _cumsum_rows(gates)
    c_q = total + jnp.sum(jnp.where(lane == 2 * H_A + h, c_col_all, 0.0), axis=1, keepdims=True)

    scale = D_B ** -0.5
    s_past = _dot_nt(q, kp_ref[0].astype(BF16)) * scale + c_q - c_past
    s_new = _dot_nt(q, kn_ref[0].astype(BF16)) * scale + c_q - c_new_row[:, 0:L]
    r = lax.broadcasted_iota(jnp.int32, (L, L), 0)
    cc = lax.broadcasted_iota(jnp.int32, (L, L), 1)
    s_new = jnp.where(cc <= r, s_new, -jnp.inf)
    m = jnp.maximum(jnp.max(s_past, axis=1, keepdims=True), jnp.max(s_new, axis=1, keepdims=True))
    p_past = jnp.exp(s_past - m)
    p_new = jnp.exp(s_new - m)
    denom = jnp.sum(p_past, axis=1, keepdims=True) + jnp.sum(p_new, axis=1, keepdims=True)
    o = _dot(p_past.astype(BF16), vp_ref[0].astype(BF16)) + _dot(p_new.astype(BF16), vn_ref[0].astype(BF16))
    o_ref[0] = o / denom


def _fox_dec(qn, kn, v, k_past, v_past, lfp_t, lfn_t, gates):
    b, l, n_b = qn.shape
    t_past = k_past.shape[1]
    new = lambda bi, h: (bi, 0, h)
    return pl.pallas_call(
        _fox_dec_kernel,
        grid=(b, H_B),
        in_specs=[pl.BlockSpec((1, l, D_B), new), pl.BlockSpec((1, l, D_B), new),
                  pl.BlockSpec((1, l, D_B), new),
                  pl.BlockSpec((1, t_past, D_B), new), pl.BlockSpec((1, t_past, D_B), new),
                  pl.BlockSpec((1, 1, 1, t_past), lambda bi, h: (bi, h, 0, 0)),
                  pl.BlockSpec((1, 1, 1, LANES), lambda bi, h: (bi, h, 0, 0)),
                  pl.BlockSpec((1, l, LANES), lambda bi, h: (bi, 0, 0))],
        out_specs=pl.BlockSpec((1, l, D_B), new),
        out_shape=jax.ShapeDtypeStruct((b, l, n_b), F32),
        compiler_params=_params("parallel", "parallel"),
        name="fox_dec",
    )(qn, kn, v, k_past, v_past, lfp_t, lfn_t, gates)


def _mlp(x, gain_ref, wu_ref, wd_ref, hid_ref):
    hn = _rms(x, gain_ref[...]).astype(BF16)
    d_ff = wu_ref.shape[1]
    ck = 1024
    for c in range(d_ff // ck):
        a = jnp.maximum(_dot(hn, wu_ref[:, c * ck:(c + 1) * ck]), 0.0)
        hid_ref[:, c * ck:(c + 1) * ck] = (a * a).astype(BF16)
    return x + _dot(hid_ref[...], wd_ref[...])


def _mix_mlp_kernel(x_ref, oa_ref, ob_ref, wo_ref, gain_ref, wu_ref, wd_ref, y_ref, hid_ref):
    n_a = oa_ref.shape[1]
    x = (x_ref[...] + _dot(oa_ref[...].astype(BF16), wo_ref[0:n_a, :])
         + _dot(ob_ref[...].astype(BF16), wo_ref[n_a:, :]))
    y_ref[...] = _mlp(x, gain_ref, wu_ref, wd_ref, hid_ref)


def _mix_mlp(x, oa, ob, wo, gain, wu, wd, tm):
    t, d = x.shape
    row = lambda i: (i, 0)
    return pl.pallas_call(
        _mix_mlp_kernel,
        grid=(t // tm,),
        in_specs=[pl.BlockSpec((tm, d), row), pl.BlockSpec((tm, oa.shape[1]), row),
                  pl.BlockSpec((tm, ob.shape[1]), row), _const_spec(wo.shape),
                  _const_spec(gain.shape), _const_spec(wu.shape), _const_spec(wd.shape)],
        out_specs=pl.BlockSpec((tm, d), row),
        out_shape=jax.ShapeDtypeStruct((t, d), F32),
        scratch_shapes=[pltpu.VMEM((tm, wu.shape[1]), BF16)],
        compiler_params=_params("parallel"),
        name="mix_mlp",
    )(x, oa, ob, wo, gain, wu, wd)


def _glu_kernel(x_ref, gain_ref, w_ref, g_ref):
    h = _rms(x_ref[...], gain_ref[...]).astype(BF16)
    d = g_ref.shape[1]
    g_ref[...] = _dot(h, w_ref[:, 0:d]) * _sigmoid(_dot(h, w_ref[:, d:2 * d]))


def _glu(x, gain, w, tm):
    t, d = x.shape
    row = lambda i: (i, 0)
    return pl.pallas_call(
        _glu_kernel,
        grid=(t // tm,),
        in_specs=[pl.BlockSpec((tm, d), row), _const_spec(gain.shape), _const_spec(w.shape)],
        out_specs=pl.BlockSpec((tm, d), row),
        out_shape=jax.ShapeDtypeStruct((t, d), F32),
        compiler_params=_params("parallel"),
        name="glu",
    )(x, gain, w)


HALO = 32


def _convmod_mlp_kernel(x_ref, g_ref, prev_ref, buf_ref, dw_ref, dwb_ref, lng_ref, lnb_ref, wpw_ref,
                        gain_ref, wu_ref, wd_ref, y_ref, xp_ref, hid_ref):
    i = pl.program_id(1)
    tm = g_ref.shape[1]

    @pl.when(i == 0)
    def _():
        xp_ref[0:HALO, :] = buf_ref[0]

    @pl.when(i > 0)
    def _():
        xp_ref[0:HALO, :] = prev_ref[0]

    xp_ref[HALO:HALO + tm, :] = g_ref[0]
    off = HALO - (CONV_C - 1)
    acc = xp_ref[off:off + tm, :] * dw_ref[0:1, :]
    for j in range(1, CONV_C):
        acc = acc + xp_ref[off + j:off + j + tm, :] * dw_ref[j:j + 1, :]
    cv = acc + dwb_ref[...]
    mu = jnp.mean(cv, axis=-1, keepdims=True)
    xc = cv - mu
    ln = xc * lax.rsqrt(jnp.mean(xc * xc, axis=-1, keepdims=True) + EPS) * lng_ref[...] + lnb_ref[...]
    act = (ln * _sigmoid(ln)).astype(BF16)
    x = x_ref[0] + _dot(act, wpw_ref[...])
    y_ref[0] = _mlp(x, gain_ref, wu_ref, wd_ref, hid_ref)


def _convmod_mlp(x, g, buf, dw, dwb, lng, lnb, wpw, gain, wu, wd, tm):
    b, l, d = x.shape
    blk = lambda bi, i: (bi, i, 0)
    per = tm // HALO
    prev = lambda bi, i: (bi, jnp.maximum(i * per - 1, 0), 0)
    return pl.pallas_call(
        _convmod_mlp_kernel,
        grid=(b, l // tm),
        in_specs=[pl.BlockSpec((1, tm, d), blk), pl.BlockSpec((1, tm, d), blk),
                  pl.BlockSpec((1, HALO, d), prev),
                  pl.BlockSpec((1, HALO, d), lambda bi, i: (bi, 0, 0)),
                  _const_spec(dw.shape), _const_spec(dwb.shape), _const_spec(lng.shape),
                  _const_spec(lnb.shape), _const_spec(wpw.shape), _const_spec(gain.shape),
                  _const_spec(wu.shape), _const_spec(wd.shape)],
        out_specs=pl.BlockSpec((1, tm, d), blk),
        out_shape=jax.ShapeDtypeStruct((b, l, d), F32),
        scratch_shapes=[pltpu.VMEM((HALO + tm, d), F32), pltpu.VMEM((tm, wu.shape[1]), BF16)],
        compiler_params=_params("parallel", "arbitrary"),
        name="convmod_mlp",
    )(x, g, g, buf, dw, dwb, lng, lnb, wpw, gain, wu, wd)


def _tile(n, pref):
    t = min(n, pref)
    assert n % t == 0
    return t


def _pad_lanes(v, offset=0):
    return jnp.zeros((1, LANES), F32).at[0, offset:offset + v.shape[0]].set(v.astype(F32))


def _trunk(x, past, w):
    b, l, d = x.shape
    t = b * l
    x2 = x.reshape(t, d)
    tm = _tile(t, 512)

    qkv, z, qn, kn, v, gates = _inproj(x2, w["norm_mix_e"], w["w_main"], w["w_gate"], w["a_log"],
                                       w["dt_bias"], w["b_f"], w["q_norm_b"], w["k_norm_b"], tm)
    n_b = H_B * D_B
    r3 = lambda a: a.reshape(b, l, a.shape[-1])
    if past is None:
        buf_a = jnp.zeros((b, CONV_A - 1, qkv.shape[-1]), F32)
        s0 = jnp.zeros((b, H_A, DK_A, DV_A), F32)
    else:
        buf_a, s0 = past["conv_a"], past["delta"]
    o_a, nbuf_a, s_a = _gdn(r3(qkv), r3(z), r3(gates), buf_a, s0, w["conv_a_w"], w["g_norm_a"])

    if past is None:
        tq = _tile(l, 256)
        qa, ka, vt = _fox_prep(r3(qn), r3(kn), r3(v), r3(gates), tq)
        o_b = _fox(qa, ka, vt)
    else:
        t_past = past["k_b"].shape[1]
        lfp_t = jnp.transpose(past["lf_b"], (0, 2, 1)).reshape(b, H_B, 1, t_past)
        lf_new = r3(gates)[:, :, 2 * H_A:2 * H_A + H_B]
        lfn_t = jnp.zeros((b, H_B, 1, LANES), F32).at[:, :, 0, :l].set(jnp.transpose(lf_new, (0, 2, 1)))
        o_b = _fox_dec(r3(qn), r3(kn), r3(v), past["k_b"].reshape(b, t_past, n_b),
                       past["v_b"].reshape(b, t_past, n_b), lfp_t, lfn_t, r3(gates))

    x2 = _mix_mlp(x2, o_a.reshape(t, -1), o_b.reshape(t, -1), w["w_out"], w["norm_mlp0"],
                  w["w_up0"], w["w_down0"], tm)

    g = _glu(x2, w["norm_mix_o"], w["w_glu"], tm)
    g3 = g.reshape(b, l, d)
    if past is None:
        buf_c = jnp.zeros((b, CONV_C - 1, d), F32)
    else:
        buf_c = past["conv_c"]
    buf_pad = jnp.concatenate([jnp.zeros((b, HALO - (CONV_C - 1), d), F32), buf_c], axis=1)
    y = _convmod_mlp(x2.reshape(b, l, d), g3, buf_pad, w["dw_c"], w["dw_c_b"], w["ln_c_g"], w["ln_c_b"],
                     w["w_pw_c"], w["norm_mlp1"], w["w_up1"], w["w_down1"], _tile(l, 512))
    nbuf_c = jnp.concatenate([buf_c, g3], axis=1)[:, l:, :] if l < CONV_C - 1 else g3[:, l - (CONV_C - 1):, :]

    new = {
        "conv_a": nbuf_a[None], "delta": s_a[None],
        "k_b": kn.reshape(1, b, l, H_B, D_B), "v_b": v.reshape(1, b, l, H_B, D_B),
        "lf_b": r3(gates)[None, :, :, 2 * H_A:2 * H_A + H_B], "conv_c": nbuf_c[None],
    }
    return y, new


def kernel(x_prompt, x_sample, state_conv_a, state_delta_a, cache_k_b, cache_v_b, cache_logf_b, state_conv_c,
           norm_mix_e, w_in_e, b_f, conv_a_w, a_log, dt_bias, g_norm_a, q_norm_b, k_norm_b, w_out_e,
           norm_mix_o, w_glu, dw_c, dw_c_b, ln_c_g, ln_c_b, w_pw_c, norm_mlp, w_up, w_down):
    n_qkv = H_A * (2 * DK_A + DV_A)
    n_z = H_A * DV_A
    n_b = H_B * D_B
    w_in = w_in_e[0]
    o_a = n_qkv + n_z
    o_q = o_a + 2 * H_A
    o_f = o_q + 3 * n_b
    w_main = jnp.concatenate([w_in[:, :o_a], w_in[:, o_q:o_f]], axis=1).astype(BF16)
    w_gate = jnp.concatenate([w_in[:, o_a:o_q], w_in[:, o_f:o_f + H_B],
                              jnp.zeros((w_in.shape[0], LANES - 2 * H_A - H_B), F32)], axis=1).astype(BF16)
    row = lambda a: a.reshape(1, -1).astype(F32)
    w = {
        "norm_mix_e": row(norm_mix_e[0]), "w_main": w_main, "w_gate": w_gate,
        "a_log": _pad_lanes(a_log[0]), "dt_bias": _pad_lanes(dt_bias[0]),
        "b_f": _pad_lanes(b_f[0], 2 * H_A),
        "q_norm_b": row(q_norm_b[0]), "k_norm_b": row(k_norm_b[0]),
        "conv_a_w": conv_a_w[0], "g_norm_a": row(g_norm_a[0]),
        "w_out": w_out_e[0].astype(BF16),
        "norm_mlp0": row(norm_mlp[0]), "w_up0": w_up[0].astype(BF16), "w_down0": w_down[0].astype(BF16),
        "norm_mix_o": row(norm_mix_o[0]), "w_glu": w_glu[0].astype(BF16),
        "dw_c": dw_c[0], "dw_c_b": row(dw_c_b[0]), "ln_c_g": row(ln_c_g[0]), "ln_c_b": row(ln_c_b[0]),
        "w_pw_c": w_pw_c[0].astype(BF16),
        "norm_mlp1": row(norm_mlp[1]), "w_up1": w_up[1].astype(BF16), "w_down1": w_down[1].astype(BF16),
    }
    y_prompt, ps = _trunk(x_prompt, None, w)
    past = {"conv_a": state_conv_a[0], "delta": state_delta_a[0], "k_b": cache_k_b[0], "v_b": cache_v_b[0],
            "lf_b": cache_logf_b[0], "conv_c": state_conv_c[0]}
    y_sample, ss = _trunk(x_sample, past, w)
    return (y_prompt, y_sample,
            ps["conv_a"], ps["delta"], ps["k_b"], ps["v_b"], ps["lf_b"], ps["conv_c"],
            ss["conv_a"], ss["delta"], ss["k_b"], ss["v_b"], ss["lf_b"], ss["conv_c"])
```

```python
import functools
import math

import jax
import jax.numpy as jnp
from jax import lax
from jax.experimental import pallas as pl
from jax.experimental.pallas import tpu as pltpu

F32 = jnp.float32
BF16 = jnp.bfloat16
EPS = 1e-6
LANES = 128
VMEM_LIMIT = 56 * 1024 * 1024

H_A = 4
DK_A = 128
DV_A = 128
CONV_A = 4
H_B = 4
D_B = 128
CONV_C = 31
CHUNK = 64
QK_AUG = 2 * D_B
LOG2E = 1.4426950408889634


def _params(*sem):
    return pltpu.CompilerParams(dimension_semantics=sem, vmem_limit_bytes=VMEM_LIMIT)


def _const_spec(shape):
    nd = len(shape)
    return pl.BlockSpec(shape, lambda *_: (0,) * nd, pipeline_mode=pl.Buffered(1))


def _rms(x, gain):
    return x * lax.rsqrt(jnp.mean(x * x, axis=-1, keepdims=True) + EPS) * gain


def _sigmoid(x):
    return 1.0 / (1.0 + jnp.exp(-x))


def _softplus(x):
    return jnp.maximum(x, 0.0) + jnp.log(1.0 + jnp.exp(-jnp.abs(x)))


def _dot(a, b):
    return jnp.dot(a, b, preferred_element_type=F32)


def _dot_nt(a, b):
    return lax.dot_general(a, b, (((1,), (1,)), ((), ())), preferred_element_type=F32)


def _dot_tn(a, b):
    return lax.dot_general(a, b, (((0,), (0,)), ((), ())), preferred_element_type=F32)


def _split2(x):
    hi = x.astype(BF16)
    lo = (x - hi.astype(F32)).astype(BF16)
    return hi, lo


def _split3(x):
    hi = x.astype(BF16)
    r = x - hi.astype(F32)
    mid = r.astype(BF16)
    lo = (r - mid.astype(F32)).astype(BF16)
    return hi, mid, lo


def _dot3(a, b):
    ah, al = _split2(a)
    bh, bl = _split2(b)
    return _dot(ah, bh) + (_dot(ah, bl) + _dot(al, bh))


def _cumsum_rows(x):
    n = x.shape[0]
    row = lax.broadcasted_iota(jnp.int32, x.shape, 0)
    s = 1
    while s < n:
        x = x + jnp.where(row >= s, pltpu.roll(x, s, axis=0), 0.0)
        s *= 2
    return x


def _cumsum_lanes(x):
    n = x.shape[1]
    col = lax.broadcasted_iota(jnp.int32, x.shape, 1)
    s = 1
    while s < n:
        x = x + jnp.where(col >= s, pltpu.roll(x, s, axis=1), 0.0)
        s *= 2
    return x


def _inproj_kernel(x_ref, gain_ref, wm_ref, wg_ref, alog_ref, dtb_ref, bf_ref, qg_ref, kg_ref,
                   qkv_ref, z_ref, qn_ref, kn_ref, v_ref, gates_ref):
    h = _rms(x_ref[...], gain_ref[...]).astype(BF16)
    n_qkv = qkv_ref.shape[1]
    n_z = z_ref.shape[1]
    n_b = qn_ref.shape[1]
    qkv_ref[...] = _dot(h, wm_ref[:, 0:n_qkv])
    z_ref[...] = _dot(h, wm_ref[:, n_qkv:n_qkv + n_z])
    off = n_qkv + n_z
    for hh in range(H_B):
        q = _dot(h, wm_ref[:, off + hh * D_B:off + (hh + 1) * D_B])
        qn_ref[:, hh * D_B:(hh + 1) * D_B] = _rms(q, qg_ref[...])
        k = _dot(h, wm_ref[:, off + n_b + hh * D_B:off + n_b + (hh + 1) * D_B])
        kn_ref[:, hh * D_B:(hh + 1) * D_B] = _rms(k, kg_ref[...])
    v_ref[...] = _dot(h, wm_ref[:, off + 2 * n_b:off + 3 * n_b])
    g = _dot(h, wg_ref[...])
    lane = lax.broadcasted_iota(jnp.int32, g.shape, 1)
    lg = -jnp.exp(alog_ref[...]) * _softplus(g + dtb_ref[...])
    beta = _sigmoid(g)
    lf = -_softplus(-(g + bf_ref[...]))
    gates_ref[...] = jnp.where(lane < H_A, lg,
                               jnp.where(lane < 2 * H_A, beta,
                                         jnp.where(lane < 2 * H_A + H_B, lf, 0.0)))


def _inproj(x, gain, wm, wg, alog, dtb, bfr, qg, kg, tm):
    t, d = x.shape
    n_qkv = H_A * (2 * DK_A + DV_A)
    n_z = H_A * DV_A
    n_b = H_B * D_B
    row = lambda i: (i, 0)
    outs = [n_qkv, n_z, n_b, n_b, n_b, LANES]
    return pl.pallas_call(
        _inproj_kernel,
        grid=(t // tm,),
        in_specs=[pl.BlockSpec((tm, d), row), _const_spec(gain.shape), _const_spec(wm.shape),
                  _const_spec(wg.shape), _const_spec(alog.shape), _const_spec(dtb.shape),
                  _const_spec(bfr.shape), _const_spec(qg.shape), _const_spec(kg.shape)],
        out_specs=[pl.BlockSpec((tm, n), row) for n in outs],
        out_shape=[jax.ShapeDtypeStruct((t, n), F32) for n in outs],
        compiler_params=_params("parallel"),
        name="inproj",
    )(x, gain, wm, wg, alog, dtb, bfr, qg, kg)


def _each(f, *lists):
    return [f(*xs) for xs in zip(*lists)]


def _mm(x, y):
    return _dot(x.astype(BF16), y.astype(BF16))


def _unit_lower_inverse_minus_eye(a_list, same_block):
    d = _each(lambda a: jnp.where(same_block, a, 0.0), a_list)
    nn = _each(lambda a, dd: a - dd, a_list, d)
    d2 = _each(lambda x: _mm(x, x), d)
    d4 = _each(lambda x: _mm(x, x), d2)
    e = _each(lambda dd, x2: x2 - dd - _mm(dd, x2), d, d2)
    d8 = _each(lambda x: _mm(x, x), d4)
    e = _each(lambda ee, x4: ee + x4 + _mm(ee, x4), e, d4)
    e = _each(lambda ee, x8: ee + x8 + _mm(ee, x8), e, d8)
    m = _each(lambda ee, n: n + _mm(ee, n), e, nn)
    m2 = _each(lambda x: _mm(x, x), m)
    ex = _each(lambda ee, mm: ee - mm - _mm(mm, ee), e, m)
    return _each(lambda x, mm2: x + mm2 + _mm(mm2, x), ex, m2)


def _gdn_kernel(qkv_ref, z_ref, gates_ref, buf0_ref, s0_ref, cw_ref, gn_ref,
                o_ref, nbuf_ref, sout_ref, xbuf, state):
    c = pl.program_id(1)
    L = CHUNK
    bb, rows = qkv_ref.shape[0], qkv_ref.shape[1]
    nc = rows // L
    pad = 8
    bf = lambda x: x.astype(BF16)

    @pl.when(c == 0)
    def _():
        xbuf[:, pad - (CONV_A - 1):pad, :] = buf0_ref[...]
        state[...] = s0_ref[...]

    xbuf[:, pad:pad + rows, :] = qkv_ref[...]
    w = cw_ref[...]
    ys = []
    for bi in range(bb):
        y = xbuf[bi, pad:pad + rows, :] * w[CONV_A - 1:CONV_A, :]
        for i in range(1, CONV_A):
            y = y + xbuf[bi, pad - i:pad - i + rows, :] * w[CONV_A - 1 - i:CONV_A - i, :]
        ys.append(y * _sigmoid(y))
    tail = xbuf[:, pad + rows - (CONV_A - 1):pad + rows, :]
    xbuf[:, pad - (CONV_A - 1):pad, :] = tail
    nbuf_ref[...] = tail

    r = lax.broadcasted_iota(jnp.int32, (L, L), 0)
    cc = lax.broadcasted_iota(jnp.int32, (L, L), 1)
    causal = r >= cc
    strict = r > cc
    same_block = (r // 16) == (cc // 16)

    units = [(bi, ci, h) for bi in range(bb) for ci in range(nc) for h in range(H_A)]
    gam = {}
    for bi in range(bb):
        for ci in range(nc):
            gates = gates_ref[bi, ci * L:(ci + 1) * L, :]
            gam_all = _cumsum_rows(gates)
            gam_t = jnp.concatenate([gam_all, jnp.zeros_like(gam_all)], axis=0).T
            gam[bi, ci] = (gates, gam_all, gam_t)

    def normalized(bi, ci, h):
        rs = slice(ci * L, (ci + 1) * L)
        q = ys[bi][rs, h * DK_A:(h + 1) * DK_A]
        k = ys[bi][rs, H_A * DK_A + h * DK_A:H_A * DK_A + (h + 1) * DK_A]
        v = ys[bi][rs, 2 * H_A * DK_A + h * DV_A:2 * H_A * DK_A + (h + 1) * DV_A]
        q = q * lax.rsqrt(jnp.sum(q * q, axis=-1, keepdims=True) + EPS) * (DK_A ** -0.5)
        k = k * lax.rsqrt(jnp.sum(k * k, axis=-1, keepdims=True) + EPS)
        return q, k, v

    qkv_n = [normalized(*u) for u in units]
    q = [t[0] for t in qkv_n]
    k = [t[1] for t in qkv_n]
    v = [t[2] for t in qkv_n]
    gcol = [gam[bi, ci][1][:, h:h + 1] for bi, ci, h in units]
    grow = [gam[bi, ci][2][h:h + 1, 0:L] for bi, ci, h in units]
    beta = [gam[bi, ci][0][:, H_A + h:H_A + h + 1] for bi, ci, h in units]
    g_last = [gam[bi, ci][1][L - 1:L, h:h + 1] for bi, ci, h in units]
    decay = _each(lambda gc, gr: jnp.exp(jnp.where(causal, gc - gr, -jnp.inf)), gcol, grow)
    eg = _each(jnp.exp, gcol)
    kb = _each(bf, k)
    qb = _each(bf, q)
    kk = _each(_dot_nt, kb, kb)
    qk = _each(_dot_nt, qb, kb)
    a_mat = _each(lambda b_, d_, kk_: jnp.where(strict, b_ * d_ * kk_, 0.0), beta, decay, kk)
    p_mat = _each(lambda d_, qk_: bf(d_ * qk_), decay, qk)
    e_inv = _unit_lower_inverse_minus_eye(a_mat, same_block)
    x = _each(lambda b_, v_, eg_, k_: jnp.concatenate([b_ * v_, (b_ * eg_) * k_], axis=1), beta, v, eg, k)
    tx = _each(lambda x_, e_: x_ + _mm(e_, x_), x, e_inv)
    u0 = [t[:, 0:DV_A] for t in tx]
    wq = _each(lambda t, eg_, q_: jnp.concatenate([bf(t[:, DV_A:]), bf(eg_ * q_)], axis=0), tx, eg, q)
    kd = _each(lambda gl, gc, k_: bf(jnp.exp(gl - gc) * k_), g_last, gcol, k)
    dec = _each(jnp.exp, g_last)

    chains = [(bi, h) for bi in range(bb) for h in range(H_A)]
    s = [state[bi, h] for bi, h in chains]
    for ci in range(nc):
        ids = [units.index((bi, ci, h)) for bi, h in chains]
        ws_qs = [_dot(wq[u], bf(s_)) for u, s_ in zip(ids, s)]
        u_b = [bf(u0[u] - t[0:L]) for u, t in zip(ids, ws_qs)]
        pu = [_dot(p_mat[u], ub_) for u, ub_ in zip(ids, u_b)]
        ku = [_dot_tn(kd[u], ub_) for u, ub_ in zip(ids, u_b)]
        s = [dec[u] * s_ + ku_ for u, s_, ku_ in zip(ids, s, ku)]
        for (bi, h), t, pu_ in zip(chains, ws_qs, pu):
            zz = z_ref[bi, ci * L:(ci + 1) * L, h * DV_A:(h + 1) * DV_A]
            o_ref[bi, ci * L:(ci + 1) * L, h * DV_A:(h + 1) * DV_A] = (
                _rms(t[L:2 * L] + pu_, gn_ref[...]) * (zz * _sigmoid(zz)))
    for (bi, h), s_ in zip(chains, s):
        state[bi, h] = s_
        sout_ref[bi, h] = s_


GDN_CHUNKS_PER_STEP = 8


def _gdn(qkv, z, gates, buf0, s0, conv_w, g_norm):
    b, l, n_qkv = qkv.shape
    nc = min(l // CHUNK, 4)
    rows = nc * CHUNK
    bb = _tile(b, max(GDN_CHUNKS_PER_STEP // nc, 1))
    n_z = z.shape[2]
    blk = lambda bi, ci: (bi, ci, 0)
    per_b3 = lambda bi, ci: (bi, 0, 0)
    per_b4 = lambda bi, ci: (bi, 0, 0, 0)
    return pl.pallas_call(
        _gdn_kernel,
        grid=(b // bb, l // rows),
        in_specs=[pl.BlockSpec((bb, rows, n_qkv), blk), pl.BlockSpec((bb, rows, n_z), blk),
                  pl.BlockSpec((bb, rows, LANES), blk),
                  pl.BlockSpec((bb, CONV_A - 1, n_qkv), per_b3),
                  pl.BlockSpec((bb, H_A, DK_A, DV_A), per_b4),
                  _const_spec(conv_w.shape), _const_spec(g_norm.shape)],
        out_specs=[pl.BlockSpec((bb, rows, n_z), blk),
                   pl.BlockSpec((bb, CONV_A - 1, n_qkv), per_b3),
                   pl.BlockSpec((bb, H_A, DK_A, DV_A), per_b4)],
        out_shape=[jax.ShapeDtypeStruct((b, l, n_z), F32),
                   jax.ShapeDtypeStruct((b, CONV_A - 1, n_qkv), F32),
                   jax.ShapeDtypeStruct((b, H_A, DK_A, DV_A), F32)],
        scratch_shapes=[pltpu.VMEM((bb, 8 + rows, n_qkv), F32), pltpu.VMEM((bb, H_A, DK_A, DV_A), F32)],
        compiler_params=_params("parallel", "arbitrary"),
        name="gdn",
    )(qkv, z, gates, buf0, s0, conv_w, g_norm)


def _fox_prep_kernel(qn_ref, kn_ref, v_ref, gates_ref, qa_ref, ka_ref, vt_ref, carry):
    i = pl.program_id(1)

    @pl.when(i == 0)
    def _():
        carry[...] = jnp.zeros_like(carry)

    tp = qn_ref.shape[1]
    csum = _cumsum_rows(gates_ref[0]) + carry[...]
    carry[...] = csum[tp - 1:tp, :]
    lane = lax.broadcasted_iota(jnp.int32, (tp, LANES), 1)
    for h in range(H_B):
        c2 = jnp.broadcast_to(csum[:, 2 * H_A + h:2 * H_A + h + 1] * LOG2E, (tp, LANES))
        hi, mid, lo = (piece.astype(F32) for piece in _split3(c2))
        q_aug = jnp.where(lane == 0, hi, jnp.where(lane == 1, mid, jnp.where(lane == 2, lo,
                          jnp.where(lane < 6, 1.0, 0.0)))).astype(BF16)
        k_aug = jnp.where(lane < 3, 1.0, jnp.where(lane == 3, -hi, jnp.where(lane == 4, -mid,
                          jnp.where(lane == 5, -lo, 0.0)))).astype(BF16)
        sl = slice(h * D_B, (h + 1) * D_B)
        qa_ref[0, h, 0, :, 0:D_B] = (qn_ref[0, :, sl] * (D_B ** -0.5 * LOG2E)).astype(BF16)
        qa_ref[0, h, 0, :, D_B:QK_AUG] = q_aug
        ka_ref[0, h, 0, :, 0:D_B] = kn_ref[0, :, sl].astype(BF16)
        ka_ref[0, h, 0, :, D_B:QK_AUG] = k_aug
        vt_ref[0, h, 0] = v_ref[0, :, sl].T.astype(BF16)


def _fox_prep(qn, kn, v, gates, tp):
    b, s, n_b = qn.shape
    nb = s // tp
    blk = lambda bi, i: (bi, i, 0)
    oblk = lambda bi, i: (bi, 0, i, 0, 0)
    return pl.pallas_call(
        _fox_prep_kernel,
        grid=(b, nb),
        in_specs=[pl.BlockSpec((1, tp, n_b), blk), pl.BlockSpec((1, tp, n_b), blk),
                  pl.BlockSpec((1, tp, n_b), blk), pl.BlockSpec((1, tp, LANES), blk)],
        out_specs=[pl.BlockSpec((1, H_B, 1, tp, QK_AUG), oblk),
                   pl.BlockSpec((1, H_B, 1, tp, QK_AUG), oblk),
                   pl.BlockSpec((1, H_B, 1, D_B, tp), oblk)],
        out_shape=[jax.ShapeDtypeStruct((b, H_B, nb, tp, QK_AUG), BF16),
                   jax.ShapeDtypeStruct((b, H_B, nb, tp, QK_AUG), BF16),
                   jax.ShapeDtypeStruct((b, H_B, nb, D_B, tp), BF16)],
        scratch_shapes=[pltpu.VMEM((1, LANES), F32)],
        compiler_params=_params("parallel", "arbitrary"),
        name="fox_prep",
    )(qn, kn, v, gates)


FOX_GROUP = 256


def _fox_kernel(qa_ref, ka_ref, vt_ref, o_ref, s_a, s_b, m_s, l_s, acc_s):
    i = pl.program_id(2)
    tq = qa_ref.shape[3]
    ng = tq // FOX_GROUP
    m_s[...] = jnp.full(m_s.shape, -jnp.inf, F32)
    l_s[...] = jnp.zeros(l_s.shape, F32)
    acc_s[...] = jnp.zeros(acc_s.shape, F32)

    def produce(j, s_ref):
        kj = ka_ref[0, 0, j]
        for g in range(ng):
            s_ref[g] = _dot_nt(kj, qa_ref[0, 0, 0, g * FOX_GROUP:(g + 1) * FOX_GROUP, :])

    def consume(j, s_ref, diagonal):
        for g in range(ng):
            rows = (g + 1) * FOX_GROUP if diagonal else s_ref.shape[1]
            s = s_ref[g, 0:rows, :]
            if diagonal:
                kpos = lax.broadcasted_iota(jnp.int32, s.shape, 0)
                qpos = lax.broadcasted_iota(jnp.int32, s.shape, 1) + g * FOX_GROUP
                s = jnp.where(kpos <= qpos, s, -jnp.inf)
            m_old = m_s[g]
            m_new = jnp.maximum(m_old, jnp.max(s, axis=0, keepdims=True))
            p = jnp.exp2(s - m_new)
            corr = jnp.exp2(m_old - m_new)
            l_s[g] = l_s[g] * corr + jnp.sum(p, axis=0, keepdims=True)
            acc_s[g] = acc_s[g] * corr + _dot(vt_ref[0, 0, j, :, 0:rows], p.astype(BF16))
            m_s[g] = m_new

    produce(0, s_a)

    def pair(t, carry):
        j = 2 * t
        produce(j + 1, s_b)
        consume(j, s_a, False)
        produce(j + 2, s_a)
        consume(j + 1, s_b, False)
        return carry

    lax.fori_loop(0, i // 2, pair, 0)

    @pl.when(i % 2 == 0)
    def _():
        consume(i, s_a, True)

    @pl.when(i % 2 == 1)
    def _():
        produce(i, s_b)
        consume(i - 1, s_a, False)
        consume(i, s_b, True)

    for g in range(ng):
        o_ref[0, g * FOX_GROUP:(g + 1) * FOX_GROUP, :] = (acc_s[g] / l_s[g]).T


def _fox(qa, ka, vt):
    b, hb, nb, tq, _ = qa.shape
    ng = tq // FOX_GROUP
    return pl.pallas_call(
        _fox_kernel,
        grid=(b, hb, nb),
        in_specs=[pl.BlockSpec((1, 1, 1, tq, QK_AUG), lambda bi, h, i: (bi, h, i, 0, 0)),
                  pl.BlockSpec((1, 1, nb, tq, QK_AUG), lambda bi, h, i: (bi, h, 0, 0, 0)),
                  pl.BlockSpec((1, 1, nb, D_B, tq), lambda bi, h, i: (bi, h, 0, 0, 0))],
        out_specs=pl.BlockSpec((1, tq, D_B), lambda bi, h, i: (bi, i, h)),
        out_shape=jax.ShapeDtypeStruct((b, nb * tq, hb * D_B), F32),
        scratch_shapes=[pltpu.VMEM((ng, tq, FOX_GROUP), F32), pltpu.VMEM((ng, tq, FOX_GROUP), F32),
                        pltpu.VMEM((ng, 1, FOX_GROUP), F32), pltpu.VMEM((ng, 1, FOX_GROUP), F32),
                        pltpu.VMEM((ng, D_B, FOX_GROUP), F32)],
        compiler_params=_params("parallel", "parallel", "arbitrary"),
        name="fox",
    )(qa, ka, vt)


def _fox_dec_kernel(q_ref, kn_ref, vn_ref, kp_ref, vp_ref, lfp_ref, lfn_ref, gates_ref, o_ref):
    h = pl.program_id(1)
    L = q_ref.shape[1]
    q = q_ref[0].astype(BF16)
    t_past = lfp_ref.shape[3]
    c_past = _cumsum_lanes(jnp.broadcast_to(lfp_ref[0, 0], (8, t_past)))[0:1, :]
    total = c_past[:, t_past - 1:t_past]
    c_new_row = total + _cumsum_lanes(jnp.broadcast_to(lfn_ref[0, 0], (8, LANES)))[0:1, :]
    gates = gates_ref[0]
    lane = lax.broadcasted_iota(jnp.int32, gates.shape, 1)
    c_col_all = _cumsum_rows(gates)
    c_q = total + jnp.sum(jnp.where(lane == 2 * H_A + h, c_col_all, 0.0), axis=1, keepdims=True)

    scale = D_B ** -0.5
    s_past = _dot_nt(q, kp_ref[0].astype(BF16)) * scale + c_q - c_past
    s_new = _dot_nt(q, kn_ref[0].astype(BF16)) * scale + c_q - c_new_row[:, 0:L]
    r = lax.broadcasted_iota(jnp.int32, (L, L), 0)
    cc = lax.broadcasted_iota(jnp.int32, (L, L), 1)
    s_new = jnp.where(cc <= r, s_new, -jnp.inf)
    m = jnp.maximum(jnp.max(s_past, axis=1, keepdims=True), jnp.max(s_new, axis=1, keepdims=True))
    p_past = jnp.exp(s_past - m)
    p_new = jnp.exp(s_new - m)
    denom = jnp.sum(p_past, axis=1, keepdims=True) + jnp.sum(p_new, axis=1, keepdims=True)
    o = _dot(p_past.astype(BF16), vp_ref[0].astype(BF16)) + _dot(p_new.astype(BF16), vn_ref[0].astype(BF16))
    o_ref[0] = o / denom


def _fox_dec(qn, kn, v, k_past, v_past, lfp_t, lfn_t, gates):
    b, l, n_b = qn.shape
    t_past = k_past.shape[1]
    new = lambda bi, h: (bi, 0, h)
    return pl.pallas_call(
        _fox_dec_kernel,
        grid=(b, H_B),
        in_specs=[pl.BlockSpec((1, l, D_B), new), pl.BlockSpec((1, l, D_B), new),
                  pl.BlockSpec((1, l, D_B), new),
                  pl.BlockSpec((1, t_past, D_B), new), pl.BlockSpec((1, t_past, D_B), new),
                  pl.BlockSpec((1, 1, 1, t_past), lambda bi, h: (bi, h, 0, 0)),
                  pl.BlockSpec((1, 1, 1, LANES), lambda bi, h: (bi, h, 0, 0)),
                  pl.BlockSpec((1, l, LANES), lambda bi, h: (bi, 0, 0))],
        out_specs=pl.BlockSpec((1, l, D_B), new),
        out_shape=jax.ShapeDtypeStruct((b, l, n_b), F32),
        compiler_params=_params("parallel", "parallel"),
        name="fox_dec",
    )(qn, kn, v, k_past, v_past, lfp_t, lfn_t, gates)


def _mlp(x, gain_ref, wu_ref, wd_ref, hid_ref):
    hn = _rms(x, gain_ref[...]).astype(BF16)
    d_ff = wu_ref.shape[1]
    ck = 1024
    for c in range(d_ff // ck):
        a = jnp.maximum(_dot(hn, wu_ref[:, c * ck:(c + 1) * ck]), 0.0)
        hid_ref[:, c * ck:(c + 1) * ck] = (a * a).astype(BF16)
    return x + _dot(hid_ref[...], wd_ref[...])


def _mix_mlp_kernel(x_ref, oa_ref, ob_ref, wo_ref, gain_ref, wu_ref, wd_ref, y_ref, hid_ref):
    n_a = oa_ref.shape[1]
    x = (x_ref[...] + _dot(oa_ref[...].astype(BF16), wo_ref[0:n_a, :])
         + _dot(ob_ref[...].astype(BF16), wo_ref[n_a:, :]))
    y_ref[...] = _mlp(x, gain_ref, wu_ref, wd_ref, hid_ref)


def _mix_mlp(x, oa, ob, wo, gain, wu, wd, tm):
    t, d = x.shape
    row = lambda i: (i, 0)
    return pl.pallas_call(
        _mix_mlp_kernel,
        grid=(t // tm,),
        in_specs=[pl.BlockSpec((tm, d), row), pl.BlockSpec((tm, oa.shape[1]), row),
                  pl.BlockSpec((tm, ob.shape[1]), row), _const_spec(wo.shape),
                  _const_spec(gain.shape), _const_spec(wu.shape), _const_spec(wd.shape)],
        out_specs=pl.BlockSpec((tm, d), row),
        out_shape=jax.ShapeDtypeStruct((t, d), F32),
        scratch_shapes=[pltpu.VMEM((tm, wu.shape[1]), BF16)],
        compiler_params=_params("parallel"),
        name="mix_mlp",
    )(x, oa, ob, wo, gain, wu, wd)


def _glu_kernel(x_ref, gain_ref, w_ref, g_ref):
    h = _rms(x_ref[...], gain_ref[...]).astype(BF16)
    d = g_ref.shape[1]
    g_ref[...] = _dot(h, w_ref[:, 0:d]) * _sigmoid(_dot(h, w_ref[:, d:2 * d]))


def _glu(x, gain, w, tm):
    t, d = x.shape
    row = lambda i: (i, 0)
    return pl.pallas_call(
        _glu_kernel,
        grid=(t // tm,),
        in_specs=[pl.BlockSpec((tm, d), row), _const_spec(gain.shape), _const_spec(w.shape)],
        out_specs=pl.BlockSpec((tm, d), row),
        out_shape=jax.ShapeDtypeStruct((t, d), F32),
        compiler_params=_params("parallel"),
        name="glu",
    )(x, gain, w)


HALO = 32


def _convmod_mlp_kernel(x_ref, g_ref, prev_ref, buf_ref, dw_ref, dwb_ref, lng_ref, lnb_ref, wpw_ref,
                        gain_ref, wu_ref, wd_ref, y_ref, xp_ref, cv_ref, hid_ref):
    i = pl.program_id(1)
    tm = g_ref.shape[1]

    @pl.when(i == 0)
    def _():
        xp_ref[0:HALO, :] = buf_ref[0]

    @pl.when(i > 0)
    def _():
        xp_ref[0:HALO, :] = prev_ref[0]

    xp_ref[HALO:HALO + tm, :] = g_ref[0]
    n = HALO + tm
    first = HALO - (CONV_C - 1)
    strip = 2 * LANES
    for cs in range(0, xp_ref.shape[1], strip):
        xp = xp_ref[:, cs:cs + strip]
        acc = None
        for r in range(8):
            xr = xp if r == 0 else pltpu.roll(xp, n - r, axis=0)
            for a in range((first + CONV_C - 1) // 8 + 1):
                j = 8 * a + r - first
                if 0 <= j < CONV_C:
                    term = xr[8 * a:8 * a + tm, :] * dw_ref[j:j + 1, cs:cs + strip]
                    acc = term if acc is None else acc + term
        cv_ref[:, cs:cs + strip] = acc
    cv = cv_ref[...] + dwb_ref[...]
    mu = jnp.mean(cv, axis=-1, keepdims=True)
    xc = cv - mu
    ln = xc * lax.rsqrt(jnp.mean(xc * xc, axis=-1, keepdims=True) + EPS) * lng_ref[...] + lnb_ref[...]
    act = (ln * _sigmoid(ln)).astype(BF16)
    x = x_ref[0] + _dot(act, wpw_ref[...])
    y_ref[0] = _mlp(x, gain_ref, wu_ref, wd_ref, hid_ref)


def _convmod_mlp(x, g, buf, dw, dwb, lng, lnb, wpw, gain, wu, wd, tm):
    b, l, d = x.shape
    blk = lambda bi, i: (bi, i, 0)
    per = tm // HALO
    prev = lambda bi, i: (bi, jnp.maximum(i * per - 1, 0), 0)
    return pl.pallas_call(
        _convmod_mlp_kernel,
        grid=(b, l // tm),
        in_specs=[pl.BlockSpec((1, tm, d), blk), pl.BlockSpec((1, tm, d), blk),
                  pl.BlockSpec((1, HALO, d), prev),
                  pl.BlockSpec((1, HALO, d), lambda bi, i: (bi, 0, 0)),
                  _const_spec(dw.shape), _const_spec(dwb.shape), _const_spec(lng.shape),
                  _const_spec(lnb.shape), _const_spec(wpw.shape), _const_spec(gain.shape),
                  _const_spec(wu.shape), _const_spec(wd.shape)],
        out_specs=pl.BlockSpec((1, tm, d), blk),
        out_shape=jax.ShapeDtypeStruct((b, l, d), F32),
        scratch_shapes=[pltpu.VMEM((HALO + tm, d), F32), pltpu.VMEM((tm, d), F32),
                        pltpu.VMEM((tm, wu.shape[1]), BF16)],
        compiler_params=_params("parallel", "arbitrary"),
        name="convmod_mlp",
    )(x, g, g, buf, dw, dwb, lng, lnb, wpw, gain, wu, wd)


def _tile(n, pref):
    t = min(n, pref)
    assert n % t == 0
    return t


def _pad_lanes(v, offset=0):
    return jnp.zeros((1, LANES), F32).at[0, offset:offset + v.shape[0]].set(v.astype(F32))


def _trunk(x, past, w):
    b, l, d = x.shape
    t = b * l
    x2 = x.reshape(t, d)
    tm = _tile(t, 512)

    qkv, z, qn, kn, v, gates = _inproj(x2, w["norm_mix_e"], w["w_main"], w["w_gate"], w["a_log"],
                                       w["dt_bias"], w["b_f"], w["q_norm_b"], w["k_norm_b"], tm)
    n_b = H_B * D_B
    r3 = lambda a: a.reshape(b, l, a.shape[-1])
    if past is None:
        buf_a = jnp.zeros((b, CONV_A - 1, qkv.shape[-1]), F32)
        s0 = jnp.zeros((b, H_A, DK_A, DV_A), F32)
    else:
        buf_a, s0 = past["conv_a"], past["delta"]
    o_a, nbuf_a, s_a = _gdn(r3(qkv), r3(z), r3(gates), buf_a, s0, w["conv_a_w"], w["g_norm_a"])

    if past is None:
        tq = _tile(l, 1024)
        qa, ka, vt = _fox_prep(r3(qn), r3(kn), r3(v), r3(gates), tq)
        o_b = _fox(qa, ka, vt)
    else:
        t_past = past["k_b"].shape[1]
        lfp_t = jnp.transpose(past["lf_b"], (0, 2, 1)).reshape(b, H_B, 1, t_past)
        lf_new = r3(gates)[:, :, 2 * H_A:2 * H_A + H_B]
        lfn_t = jnp.zeros((b, H_B, 1, LANES), F32).at[:, :, 0, :l].set(jnp.transpose(lf_new, (0, 2, 1)))
        o_b = _fox_dec(r3(qn), r3(kn), r3(v), past["k_b"].reshape(b, t_past, n_b),
                       past["v_b"].reshape(b, t_past, n_b), lfp_t, lfn_t, r3(gates))

    x2 = _mix_mlp(x2, o_a.reshape(t, -1), o_b.reshape(t, -1), w["w_out"], w["norm_mlp0"],
                  w["w_up0"], w["w_down0"], tm)

    g = _glu(x2, w["norm_mix_o"], w["w_glu"], tm)
    g3 = g.reshape(b, l, d)
    if past is None:
        buf_c = jnp.zeros((b, CONV_C - 1, d), F32)
    else:
        buf_c = past["conv_c"]
    buf_pad = jnp.concatenate([jnp.zeros((b, HALO - (CONV_C - 1), d), F32), buf_c], axis=1)
    y = _convmod_mlp(x2.reshape(b, l, d), g3, buf_pad, w["dw_c"], w["dw_c_b"], w["ln_c_g"], w["ln_c_b"],
                     w["w_pw_c"], w["norm_mlp1"], w["w_up1"], w["w_down1"], _tile(l, 512))
    nbuf_c = jnp.concatenate([buf_c, g3], axis=1)[:, l:, :] if l < CONV_C - 1 else g3[:, l - (CONV_C - 1):, :]

    new = {
        "conv_a": nbuf_a[None], "delta": s_a[None],
        "k_b": kn.reshape(1, b, l, H_B, D_B), "v_b": v.reshape(1, b, l, H_B, D_B),
        "lf_b": r3(gates)[None, :, :, 2 * H_A:2 * H_A + H_B], "conv_c": nbuf_c[None],
    }
    return y, new


def kernel(x_prompt, x_sample, state_conv_a, state_delta_a, cache_k_b, cache_v_b, cache_logf_b, state_conv_c,
           norm_mix_e, w_in_e, b_f, conv_a_w, a_log, dt_bias, g_norm_a, q_norm_b, k_norm_b, w_out_e,
           norm_mix_o, w_glu, dw_c, dw_c_b, ln_c_g, ln_c_b, w_pw_c, norm_mlp, w_up, w_down):
    n_qkv = H_A * (2 * DK_A + DV_A)
    n_z = H_A * DV_A
    n_b = H_B * D_B
    w_in = w_in_e[0]
    o_a = n_qkv + n_z
    o_q = o_a + 2 * H_A
    o_f = o_q + 3 * n_b
    w_main = jnp.concatenate([w_in[:, :o_a], w_in[:, o_q:o_f]], axis=1).astype(BF16)
    w_gate = jnp.concatenate([w_in[:, o_a:o_q], w_in[:, o_f:o_f + H_B],
                              jnp.zeros((w_in.shape[0], LANES - 2 * H_A - H_B), F32)], axis=1).astype(BF16)
    row = lambda a: a.reshape(1, -1).astype(F32)
    w = {
        "norm_mix_e": row(norm_mix_e[0]), "w_main": w_main, "w_gate": w_gate,
        "a_log": _pad_lanes(a_log[0]), "dt_bias": _pad_lanes(dt_bias[0]),
        "b_f": _pad_lanes(b_f[0], 2 * H_A),
        "q_norm_b": row(q_norm_b[0]), "k_norm_b": row(k_norm_b[0]),
        "conv_a_w": conv_a_w[0], "g_norm_a": row(g_norm_a[0]),
        "w_out": w_out_e[0].astype(BF16),
        "norm_mlp0": row(norm_mlp[0]), "w_up0": w_up[0].astype(BF16), "w_down0": w_down[0].astype(BF16),
        "norm_mix_o": row(norm_mix_o[0]), "w_glu": w_glu[0].astype(BF16),
        "dw_c": dw_c[0], "dw_c_b": row(dw_c_b[0]), "ln_c_g": row(ln_c_g[0]), "ln_c_b": row(ln_c_b[0]),
        "w_pw_c": w_pw_c[0].astype(BF16),
        "norm_mlp1": row(norm_mlp[1]), "w_up1": w_up[1].astype(BF16), "w_down1": w_down[1].astype(BF16),
    }
    y_prompt, ps = _trunk(x_prompt, None, w)
    past = {"conv_a": state_conv_a[0], "delta": state_delta_a[0], "k_b": cache_k_b[0], "v_b": cache_v_b[0],
            "lf_b": cache_logf_b[0], "conv_c": state_conv_c[0]}
    y_sample, ss = _trunk(x_sample, past, w)
    return (y_prompt, y_sample,
            ps["conv_a"], ps["delta"], ps["k_b"], ps["v_b"], ps["lf_b"], ps["conv_c"],
            ss["conv_a"], ss["delta"], ss["k_b"], ss["v_b"], ss["lf_b"], ss["conv_c"])
```

```python
import functools
import math

import jax
import jax.numpy as jnp
from jax import lax
from jax.experimental import pallas as pl
from jax.experimental.pallas import tpu as pltpu

F32 = jnp.float32
BF16 = jnp.bfloat16
EPS = 1e-6
LANES = 128
VMEM_LIMIT = 56 * 1024 * 1024

H_A = 4
DK_A = 128
DV_A = 128
CONV_A = 4
H_B = 4
D_B = 128
CONV_C = 31
CHUNK = 64
QK_AUG = 2 * D_B
LOG2E = 1.4426950408889634


def _params(*sem, flags=None):
    return pltpu.CompilerParams(dimension_semantics=sem, vmem_limit_bytes=VMEM_LIMIT, flags=flags)


def _const_spec(shape):
    nd = len(shape)
    return pl.BlockSpec(shape, lambda *_: (0,) * nd, pipeline_mode=pl.Buffered(1))


def _rms(x, gain):
    return x * lax.rsqrt(jnp.mean(x * x, axis=-1, keepdims=True) + EPS) * gain


def _sigmoid(x):
    return 1.0 / (1.0 + jnp.exp(-x))


def _softplus(x):
    return jnp.maximum(x, 0.0) + jnp.log(1.0 + jnp.exp(-jnp.abs(x)))


def _dot(a, b):
    return jnp.dot(a, b, preferred_element_type=F32)


def _dot_nt(a, b):
    return lax.dot_general(a, b, (((1,), (1,)), ((), ())), preferred_element_type=F32)


def _dot_tn(a, b):
    return lax.dot_general(a, b, (((0,), (0,)), ((), ())), preferred_element_type=F32)


def _split3(x):
    hi = x.astype(BF16)
    r = x - hi.astype(F32)
    mid = r.astype(BF16)
    lo = (r - mid.astype(F32)).astype(BF16)
    return hi, mid, lo


def _cumsum_rows(x):
    n = x.shape[0]
    row = lax.broadcasted_iota(jnp.int32, x.shape, 0)
    s = 1
    while s < n:
        x = x + jnp.where(row >= s, pltpu.roll(x, s, axis=0), 0.0)
        s *= 2
    return x


def _cumsum_lanes(x):
    n = x.shape[1]
    col = lax.broadcasted_iota(jnp.int32, x.shape, 1)
    s = 1
    while s < n:
        x = x + jnp.where(col >= s, pltpu.roll(x, s, axis=1), 0.0)
        s *= 2
    return x


def _inproj_kernel(x_ref, gain_ref, wm_ref, alog_ref, dtb_ref, bf_ref, qg_ref, kg_ref,
                   qkv_ref, z_ref, qn_ref, kn_ref, v_ref, gates_ref):
    h = _rms(x_ref[...], gain_ref[...]).astype(BF16)
    n_qkv = qkv_ref.shape[1]
    n_z = z_ref.shape[1]
    n_b = qn_ref.shape[1]
    qkv_ref[...] = _dot(h, wm_ref[:, 0:n_qkv])
    z_ref[...] = _dot(h, wm_ref[:, n_qkv:n_qkv + n_z])
    off = n_qkv + n_z
    q = _dot(h, wm_ref[:, off:off + n_b])
    k = _dot(h, wm_ref[:, off + n_b:off + 2 * n_b])
    vg = _dot(h, wm_ref[:, off + 2 * n_b:off + 3 * n_b + LANES])
    for hh in range(H_B):
        sl = slice(hh * D_B, (hh + 1) * D_B)
        qn_ref[:, sl] = _rms(q[:, sl], qg_ref[...])
        kn_ref[:, sl] = _rms(k[:, sl], kg_ref[...])
    v_ref[...] = vg[:, 0:n_b]
    g = vg[:, n_b:n_b + LANES]
    lane = lax.broadcasted_iota(jnp.int32, g.shape, 1)
    lg = -jnp.exp(alog_ref[...]) * _softplus(g + dtb_ref[...])
    beta = _sigmoid(g)
    lf = -_softplus(-(g + bf_ref[...]))
    gates_ref[...] = jnp.where(lane < H_A, lg,
                               jnp.where(lane < 2 * H_A, beta,
                                         jnp.where(lane < 2 * H_A + H_B, lf, 0.0)))


def _inproj(x, gain, wm, alog, dtb, bfr, qg, kg, tm):
    t, d = x.shape
    n_qkv = H_A * (2 * DK_A + DV_A)
    n_z = H_A * DV_A
    n_b = H_B * D_B
    row = lambda i: (i, 0)
    outs = [n_qkv, n_z, n_b, n_b, n_b, LANES]
    return pl.pallas_call(
        _inproj_kernel,
        grid=(t // tm,),
        in_specs=[pl.BlockSpec((tm, d), row), _const_spec(gain.shape), _const_spec(wm.shape),
                  _const_spec(alog.shape), _const_spec(dtb.shape),
                  _const_spec(bfr.shape), _const_spec(qg.shape), _const_spec(kg.shape)],
        out_specs=[pl.BlockSpec((tm, n), row) for n in outs],
        out_shape=[jax.ShapeDtypeStruct((t, n), F32) for n in outs],
        compiler_params=_params("parallel"),
        name="inproj",
    )(x, gain, wm, alog, dtb, bfr, qg, kg)


def _each(f, *lists):
    return [f(*xs) for xs in zip(*lists)]


def _mm(x, y):
    return _dot(x.astype(BF16), y.astype(BF16))


def _unit_lower_inverse_minus_eye(a_list, same_block):
    d = _each(lambda a: jnp.where(same_block, a, 0.0), a_list)
    nn = _each(lambda a, dd: a - dd, a_list, d)
    d2 = _each(lambda x: _mm(x, x), d)
    d4 = _each(lambda x: _mm(x, x), d2)
    e = _each(lambda dd, x2: x2 - dd - _mm(dd, x2), d, d2)
    d8 = _each(lambda x: _mm(x, x), d4)
    e = _each(lambda ee, x4: ee + x4 + _mm(ee, x4), e, d4)
    e = _each(lambda ee, x8: ee + x8 + _mm(ee, x8), e, d8)
    m = _each(lambda ee, n: n + _mm(ee, n), e, nn)
    m2 = _each(lambda x: _mm(x, x), m)
    ex = _each(lambda ee, mm: ee - mm - _mm(mm, ee), e, m)
    return _each(lambda x, mm2: x + mm2 + _mm(mm2, x), ex, m2)


def _gdn_kernel(qkv_ref, z_ref, gates_ref, buf0_ref, s0_ref, cw_ref, gn_ref,
                o_ref, nbuf_ref, sout_ref, xbuf, state):
    c = pl.program_id(1)
    L = CHUNK
    bb, rows = qkv_ref.shape[0], qkv_ref.shape[1]
    nc = rows // L
    pad = 8
    bf = lambda x: x.astype(BF16)

    @pl.when(c == 0)
    def _():
        xbuf[:, pad - (CONV_A - 1):pad, :] = buf0_ref[...]
        state[...] = s0_ref[...]

    xbuf[:, pad:pad + rows, :] = qkv_ref[...]
    w = cw_ref[...]
    ys = []
    for bi in range(bb):
        y = xbuf[bi, pad:pad + rows, :] * w[CONV_A - 1:CONV_A, :]
        for i in range(1, CONV_A):
            y = y + xbuf[bi, pad - i:pad - i + rows, :] * w[CONV_A - 1 - i:CONV_A - i, :]
        ys.append(y * _sigmoid(y))
    tail = xbuf[:, pad + rows - (CONV_A - 1):pad + rows, :]
    xbuf[:, pad - (CONV_A - 1):pad, :] = tail
    nbuf_ref[...] = tail

    r = lax.broadcasted_iota(jnp.int32, (L, L), 0)
    cc = lax.broadcasted_iota(jnp.int32, (L, L), 1)
    causal = r >= cc
    strict = r > cc
    same_block = (r // 16) == (cc // 16)

    units = [(bi, ci, h) for bi in range(bb) for ci in range(nc) for h in range(H_A)]
    gam = {}
    for bi in range(bb):
        for ci in range(nc):
            gates = gates_ref[bi, ci * L:(ci + 1) * L, :]
            gam_all = _cumsum_rows(gates)
            gam_t = jnp.concatenate([gam_all, jnp.zeros_like(gam_all)], axis=0).T
            gam[bi, ci] = (gates, gam_all, gam_t)

    def normalized(bi, ci, h):
        rs = slice(ci * L, (ci + 1) * L)
        q = ys[bi][rs, h * DK_A:(h + 1) * DK_A]
        k = ys[bi][rs, H_A * DK_A + h * DK_A:H_A * DK_A + (h + 1) * DK_A]
        v = ys[bi][rs, 2 * H_A * DK_A + h * DV_A:2 * H_A * DK_A + (h + 1) * DV_A]
        q = q * lax.rsqrt(jnp.sum(q * q, axis=-1, keepdims=True) + EPS) * (DK_A ** -0.5)
        k = k * lax.rsqrt(jnp.sum(k * k, axis=-1, keepdims=True) + EPS)
        return q, k, v

    qkv_n = [normalized(*u) for u in units]
    q = [t[0] for t in qkv_n]
    k = [t[1] for t in qkv_n]
    v = [t[2] for t in qkv_n]
    gcol = [gam[bi, ci][1][:, h:h + 1] for bi, ci, h in units]
    grow = [gam[bi, ci][2][h:h + 1, 0:L] for bi, ci, h in units]
    beta = [gam[bi, ci][0][:, H_A + h:H_A + h + 1] for bi, ci, h in units]
    g_last = [gam[bi, ci][1][L - 1:L, h:h + 1] for bi, ci, h in units]
    decay = _each(lambda gc, gr: jnp.exp(jnp.where(causal, gc - gr, -jnp.inf)), gcol, grow)
    eg = _each(jnp.exp, gcol)
    kb = _each(bf, k)
    qb = _each(bf, q)
    kk = _each(_dot_nt, kb, kb)
    qk = _each(_dot_nt, qb, kb)
    a_mat = _each(lambda b_, d_, kk_: jnp.where(strict, b_ * d_ * kk_, 0.0), beta, decay, kk)
    p_mat = _each(lambda d_, qk_: bf(d_ * qk_), decay, qk)
    e_inv = _unit_lower_inverse_minus_eye(a_mat, same_block)
    x = _each(lambda b_, v_, eg_, k_: jnp.concatenate([b_ * v_, (b_ * eg_) * k_], axis=1), beta, v, eg, k)
    tx = _each(lambda x_, e_: x_ + _mm(e_, x_), x, e_inv)
    u0 = [t[:, 0:DV_A] for t in tx]
    wq = _each(lambda t, eg_, q_: jnp.concatenate([bf(t[:, DV_A:]), bf(eg_ * q_)], axis=0), tx, eg, q)
    kd = _each(lambda gl, gc, k_: bf(jnp.exp(gl - gc) * k_), g_last, gcol, k)
    dec = _each(jnp.exp, g_last)

    chains = [(bi, h) for bi in range(bb) for h in range(H_A)]
    s = [state[bi, h] for bi, h in chains]
    for ci in range(nc):
        ids = [units.index((bi, ci, h)) for bi, h in chains]
        ws_qs = [_dot(wq[u], bf(s_)) for u, s_ in zip(ids, s)]
        u_b = [bf(u0[u] - t[0:L]) for u, t in zip(ids, ws_qs)]
        pu = [_dot(p_mat[u], ub_) for u, ub_ in zip(ids, u_b)]
        ku = [_dot_tn(kd[u], ub_) for u, ub_ in zip(ids, u_b)]
        s = [dec[u] * s_ + ku_ for u, s_, ku_ in zip(ids, s, ku)]
        for (bi, h), t, pu_ in zip(chains, ws_qs, pu):
            zz = z_ref[bi, ci * L:(ci + 1) * L, h * DV_A:(h + 1) * DV_A]
            o_ref[bi, ci * L:(ci + 1) * L, h * DV_A:(h + 1) * DV_A] = (
                _rms(t[L:2 * L] + pu_, gn_ref[...]) * (zz * _sigmoid(zz)))
    for (bi, h), s_ in zip(chains, s):
        state[bi, h] = s_
        sout_ref[bi, h] = s_


GDN_CHUNKS_PER_STEP = 8


def _gdn(qkv, z, gates, buf0, s0, conv_w, g_norm):
    b, l, n_qkv = qkv.shape
    nc = min(l // CHUNK, 4)
    rows = nc * CHUNK
    bb = _tile(b, max(GDN_CHUNKS_PER_STEP // nc, 1))
    n_z = z.shape[2]
    blk = lambda bi, ci: (bi, ci, 0)
    per_b3 = lambda bi, ci: (bi, 0, 0)
    per_b4 = lambda bi, ci: (bi, 0, 0, 0)
    return pl.pallas_call(
        _gdn_kernel,
        grid=(b // bb, l // rows),
        in_specs=[pl.BlockSpec((bb, rows, n_qkv), blk), pl.BlockSpec((bb, rows, n_z), blk),
                  pl.BlockSpec((bb, rows, LANES), blk),
                  pl.BlockSpec((bb, CONV_A - 1, n_qkv), per_b3),
                  pl.BlockSpec((bb, H_A, DK_A, DV_A), per_b4),
                  _const_spec(conv_w.shape), _const_spec(g_norm.shape)],
        out_specs=[pl.BlockSpec((bb, rows, n_z), blk),
                   pl.BlockSpec((bb, CONV_A - 1, n_qkv), per_b3),
                   pl.BlockSpec((bb, H_A, DK_A, DV_A), per_b4)],
        out_shape=[jax.ShapeDtypeStruct((b, l, n_z), F32),
                   jax.ShapeDtypeStruct((b, CONV_A - 1, n_qkv), F32),
                   jax.ShapeDtypeStruct((b, H_A, DK_A, DV_A), F32)],
        scratch_shapes=[pltpu.VMEM((bb, 8 + rows, n_qkv), F32), pltpu.VMEM((bb, H_A, DK_A, DV_A), F32)],
        compiler_params=_params("parallel", "arbitrary"),
        name="gdn",
    )(qkv, z, gates, buf0, s0, conv_w, g_norm)


def _fox_prep_kernel(qn_ref, kn_ref, v_ref, gates_ref, qa_ref, ka_ref, vt_ref, carry):
    i = pl.program_id(1)

    @pl.when(i == 0)
    def _():
        carry[...] = jnp.zeros_like(carry)

    tp = qn_ref.shape[1]
    csum = _cumsum_rows(gates_ref[0]) + carry[...]
    carry[...] = csum[tp - 1:tp, :]
    lane = lax.broadcasted_iota(jnp.int32, (tp, LANES), 1)
    for h in range(H_B):
        c2 = jnp.broadcast_to(csum[:, 2 * H_A + h:2 * H_A + h + 1] * LOG2E, (tp, LANES))
        hi, mid, lo = (piece.astype(F32) for piece in _split3(c2))
        q_aug = jnp.where(lane == 0, hi, jnp.where(lane == 1, mid, jnp.where(lane == 2, lo,
                          jnp.where(lane < 6, 1.0, 0.0)))).astype(BF16)
        k_aug = jnp.where(lane < 3, 1.0, jnp.where(lane == 3, -hi, jnp.where(lane == 4, -mid,
                          jnp.where(lane == 5, -lo, 0.0)))).astype(BF16)
        sl = slice(h * D_B, (h + 1) * D_B)
        qa_ref[0, h, 0, :, 0:D_B] = (qn_ref[0, :, sl] * (D_B ** -0.5 * LOG2E)).astype(BF16)
        qa_ref[0, h, 0, :, D_B:QK_AUG] = q_aug
        ka_ref[0, h, 0, :, 0:D_B] = kn_ref[0, :, sl].astype(BF16)
        ka_ref[0, h, 0, :, D_B:QK_AUG] = k_aug
        vt_ref[0, h, 0] = v_ref[0, :, sl].T.astype(BF16)


def _fox_prep(qn, kn, v, gates, tp):
    b, s, n_b = qn.shape
    nb = s // tp
    blk = lambda bi, i: (bi, i, 0)
    oblk = lambda bi, i: (bi, 0, i, 0, 0)
    return pl.pallas_call(
        _fox_prep_kernel,
        grid=(b, nb),
        in_specs=[pl.BlockSpec((1, tp, n_b), blk), pl.BlockSpec((1, tp, n_b), blk),
                  pl.BlockSpec((1, tp, n_b), blk), pl.BlockSpec((1, tp, LANES), blk)],
        out_specs=[pl.BlockSpec((1, H_B, 1, tp, QK_AUG), oblk),
                   pl.BlockSpec((1, H_B, 1, tp, QK_AUG), oblk),
                   pl.BlockSpec((1, H_B, 1, D_B, tp), oblk)],
        out_shape=[jax.ShapeDtypeStruct((b, H_B, nb, tp, QK_AUG), BF16),
                   jax.ShapeDtypeStruct((b, H_B, nb, tp, QK_AUG), BF16),
                   jax.ShapeDtypeStruct((b, H_B, nb, D_B, tp), BF16)],
        scratch_shapes=[pltpu.VMEM((1, LANES), F32)],
        compiler_params=_params("parallel", "arbitrary"),
        name="fox_prep",
    )(qn, kn, v, gates)


FOX_GROUP = 256


def _fox_kernel(qa_ref, ka_ref, vt_ref, o_ref, s_a, s_b, m_s, l_s, acc_s):
    i = pl.program_id(2)
    tq = qa_ref.shape[3]
    ng = tq // FOX_GROUP
    m_s[...] = jnp.full(m_s.shape, -jnp.inf, F32)
    l_s[...] = jnp.zeros(l_s.shape, F32)
    acc_s[...] = jnp.zeros(acc_s.shape, F32)

    def produce(j, s_ref):
        kj = ka_ref[0, 0, j]
        for g in range(ng):
            s_ref[g] = _dot_nt(kj, qa_ref[0, 0, 0, g * FOX_GROUP:(g + 1) * FOX_GROUP, :])

    def consume(j, s_ref, diagonal):
        for g in range(ng):
            rows = (g + 1) * FOX_GROUP if diagonal else s_ref.shape[1]
            s = s_ref[g, 0:rows, :]
            if diagonal:
                kpos = lax.broadcasted_iota(jnp.int32, s.shape, 0)
                qpos = lax.broadcasted_iota(jnp.int32, s.shape, 1) + g * FOX_GROUP
                s = jnp.where(kpos <= qpos, s, -jnp.inf)
            m_old = m_s[g]
            m_new = jnp.maximum(m_old, jnp.max(s, axis=0, keepdims=True))
            p = jnp.exp2(s - m_new)
            corr = jnp.exp2(m_old - m_new)
            l_s[g] = l_s[g] * corr + jnp.sum(p, axis=0, keepdims=True)
            acc_s[g] = acc_s[g] * corr + _dot(vt_ref[0, 0, j, :, 0:rows], p.astype(BF16))
            m_s[g] = m_new

    produce(0, s_a)

    def pair(t, carry):
        j = 2 * t
        produce(j + 1, s_b)
        consume(j, s_a, False)
        produce(j + 2, s_a)
        consume(j + 1, s_b, False)
        return carry

    lax.fori_loop(0, i // 2, pair, 0)

    @pl.when(i % 2 == 0)
    def _():
        consume(i, s_a, True)

    @pl.when(i % 2 == 1)
    def _():
        produce(i, s_b)
        consume(i - 1, s_a, False)
        consume(i, s_b, True)

    for g in range(ng):
        o_ref[0, g * FOX_GROUP:(g + 1) * FOX_GROUP, :] = (acc_s[g] / l_s[g]).T


def _fox(qa, ka, vt):
    b, hb, nb, tq, _ = qa.shape
    ng = tq // FOX_GROUP
    return pl.pallas_call(
        _fox_kernel,
        grid=(b, hb, nb),
        in_specs=[pl.BlockSpec((1, 1, 1, tq, QK_AUG), lambda bi, h, i: (bi, h, i, 0, 0)),
                  pl.BlockSpec((1, 1, nb, tq, QK_AUG), lambda bi, h, i: (bi, h, 0, 0, 0)),
                  pl.BlockSpec((1, 1, nb, D_B, tq), lambda bi, h, i: (bi, h, 0, 0, 0))],
        out_specs=pl.BlockSpec((1, tq, D_B), lambda bi, h, i: (bi, i, h)),
        out_shape=jax.ShapeDtypeStruct((b, nb * tq, hb * D_B), F32),
        scratch_shapes=[pltpu.VMEM((ng, tq, FOX_GROUP), F32), pltpu.VMEM((ng, tq, FOX_GROUP), F32),
                        pltpu.VMEM((ng, 1, FOX_GROUP), F32), pltpu.VMEM((ng, 1, FOX_GROUP), F32),
                        pltpu.VMEM((ng, D_B, FOX_GROUP), F32)],
        compiler_params=_params("parallel", "parallel", "arbitrary"),
        name="fox",
    )(qa, ka, vt)


def _fox_dec_kernel(q_ref, kn_ref, vn_ref, kp_ref, vp_ref, lfp_ref, lfn_ref, gates_ref, o_ref):
    L = q_ref.shape[1]
    t_past = lfp_ref.shape[2]
    bf = lambda x: x.astype(BF16)
    heads = range(H_B)
    c_past = _cumsum_lanes(lfp_ref[0])
    total = c_past[:, t_past - 1:t_past]
    c_new = total + _cumsum_lanes(lfn_ref[0])
    c_col = _cumsum_rows(gates_ref[0])
    c_q = [total[h:h + 1, :] + c_col[:, 2 * H_A + h:2 * H_A + h + 1] for h in heads]

    scale = D_B ** -0.5
    r = lax.broadcasted_iota(jnp.int32, (L, L), 0)
    cc = lax.broadcasted_iota(jnp.int32, (L, L), 1)
    hs = [slice(h * D_B, (h + 1) * D_B) for h in heads]
    q = [bf(q_ref[0, :, hs[h]]) for h in heads]
    s_past = [_dot_nt(q[h], bf(kp_ref[0, :, hs[h]])) for h in heads]
    s_new = [_dot_nt(q[h], bf(kn_ref[0, :, hs[h]])) for h in heads]
    s_past = [s_past[h] * scale + c_q[h] - c_past[h:h + 1, :] for h in heads]
    s_new = [jnp.where(cc <= r, s_new[h] * scale + c_q[h] - c_new[h:h + 1, 0:L], -jnp.inf) for h in heads]
    m = [jnp.maximum(jnp.max(s_past[h], axis=1, keepdims=True), jnp.max(s_new[h], axis=1, keepdims=True))
         for h in heads]
    p_past = [jnp.exp(s_past[h] - m[h]) for h in heads]
    p_new = [jnp.exp(s_new[h] - m[h]) for h in heads]
    denom = [jnp.sum(p_past[h], axis=1, keepdims=True) + jnp.sum(p_new[h], axis=1, keepdims=True) for h in heads]
    o = [_dot(bf(p_past[h]), bf(vp_ref[0, :, hs[h]])) + _dot(bf(p_new[h]), bf(vn_ref[0, :, hs[h]])) for h in heads]
    for h in heads:
        o_ref[0, :, hs[h]] = o[h] / denom[h]


def _fox_dec(qn, kn, v, k_past, v_past, lfp, lfn, gates):
    b, l, n_b = qn.shape
    t_past = k_past.shape[1]
    per3 = lambda bi: (bi, 0, 0)
    return pl.pallas_call(
        _fox_dec_kernel,
        grid=(b,),
        in_specs=[pl.BlockSpec((1, l, n_b), per3), pl.BlockSpec((1, l, n_b), per3),
                  pl.BlockSpec((1, l, n_b), per3),
                  pl.BlockSpec((1, t_past, n_b), per3), pl.BlockSpec((1, t_past, n_b), per3),
                  pl.BlockSpec((1, 8, t_past), per3), pl.BlockSpec((1, 8, LANES), per3),
                  pl.BlockSpec((1, l, LANES), per3)],
        out_specs=pl.BlockSpec((1, l, n_b), per3),
        out_shape=jax.ShapeDtypeStruct((b, l, n_b), F32),
        compiler_params=_params("parallel"),
        name="fox_dec",
    )(qn, kn, v, k_past, v_past, lfp, lfn, gates)


def _mlp(x, gain_ref, wu_ref, wd_ref, hid_ref):
    hn = _rms(x, gain_ref[...]).astype(BF16)
    d_ff = wu_ref.shape[1]
    ck = 1024
    for c in range(d_ff // ck):
        a = jnp.maximum(_dot(hn, wu_ref[:, c * ck:(c + 1) * ck]), 0.0)
        hid_ref[:, c * ck:(c + 1) * ck] = (a * a).astype(BF16)
    return x + _dot(hid_ref[...], wd_ref[...])


def _mix_mlp_kernel(x_ref, oa_ref, ob_ref, wo_ref, gain_ref, wu_ref, wd_ref, y_ref, hid_ref):
    n_a = oa_ref.shape[1]
    x = (x_ref[...] + _dot(oa_ref[...].astype(BF16), wo_ref[0:n_a, :])
         + _dot(ob_ref[...].astype(BF16), wo_ref[n_a:, :]))
    y_ref[...] = _mlp(x, gain_ref, wu_ref, wd_ref, hid_ref)


def _mix_mlp(x, oa, ob, wo, gain, wu, wd, tm):
    t, d = x.shape
    row = lambda i: (i, 0)
    return pl.pallas_call(
        _mix_mlp_kernel,
        grid=(t // tm,),
        in_specs=[pl.BlockSpec((tm, d), row), pl.BlockSpec((tm, oa.shape[1]), row),
                  pl.BlockSpec((tm, ob.shape[1]), row), _const_spec(wo.shape),
                  _const_spec(gain.shape), _const_spec(wu.shape), _const_spec(wd.shape)],
        out_specs=pl.BlockSpec((tm, d), row),
        out_shape=jax.ShapeDtypeStruct((t, d), F32),
        scratch_shapes=[pltpu.VMEM((tm, wu.shape[1]), BF16)],
        compiler_params=_params("parallel"),
        name="mix_mlp",
    )(x, oa, ob, wo, gain, wu, wd)


def _glu_kernel(x_ref, gain_ref, w_ref, g_ref):
    h = _rms(x_ref[...], gain_ref[...]).astype(BF16)
    d = g_ref.shape[1]
    g_ref[...] = _dot(h, w_ref[:, 0:d]) * _sigmoid(_dot(h, w_ref[:, d:2 * d]))


def _glu(x, gain, w, tm):
    t, d = x.shape
    row = lambda i: (i, 0)
    return pl.pallas_call(
        _glu_kernel,
        grid=(t // tm,),
        in_specs=[pl.BlockSpec((tm, d), row), _const_spec(gain.shape), _const_spec(w.shape)],
        out_specs=pl.BlockSpec((tm, d), row),
        out_shape=jax.ShapeDtypeStruct((t, d), F32),
        compiler_params=_params("parallel"),
        name="glu",
    )(x, gain, w)


HALO = 32


def _convmod_mlp_kernel(x_ref, g_ref, prev_ref, buf_ref, dw_ref, dwb_ref, lng_ref, lnb_ref, wpw_ref,
                        gain_ref, wu_ref, wd_ref, y_ref, xp_ref, cv_ref, act_ref, hid_ref, *, nt, ntiles):
    step = pl.program_id(0)
    i = jnp.minimum(step, ntiles - 1) % nt
    bb, tm, d = g_ref.shape
    rows = bb * tm

    @pl.when(step == 0)
    def _():
        act_ref[...] = jnp.zeros_like(act_ref)

    x = x_ref[...].reshape(rows, d) + _dot(act_ref[...], wpw_ref[...])
    hn = _rms(x, gain_ref[...]).astype(BF16)

    xp_ref[:, 0:HALO, :] = jnp.where(i == 0, buf_ref[...], prev_ref[...])
    xp_ref[:, HALO:HALO + tm, :] = g_ref[...]
    n = HALO + tm
    first = HALO - (CONV_C - 1)
    n_strips = wu_ref.shape[1] // 1024
    strip = d // n_strips
    never = step < 0

    def after(value, witness):
        w = witness.shape[1]
        head = jnp.where(never, witness.astype(value.dtype), value[:, 0:w])
        return head if w == value.shape[1] else jnp.concatenate([head, value[:, w:]], axis=1)

    strips = []
    for c in range(n_strips):
        cs = c * strip
        hn_c = hn if c < 2 else after(hn, strips[c - 2])
        up = jnp.maximum(_dot(hn_c, wu_ref[:, c * 1024:(c + 1) * 1024]), 0.0)
        hid_ref[:, c * 1024:(c + 1) * 1024] = (up * up).astype(BF16)
        done = []
        for bi in range(bb):
            xp = xp_ref[bi, :, cs:cs + strip]
            acc = None
            for r in range(8):
                xr = xp if r == 0 else pltpu.roll(xp, n - r, axis=0)
                for a in range((first + CONV_C - 1) // 8 + 1):
                    j = 8 * a + r - first
                    if 0 <= j < CONV_C:
                        term = xr[8 * a:8 * a + tm, :] * dw_ref[j:j + 1, cs:cs + strip]
                        acc = term if acc is None else acc + term
            cv_ref[bi * tm:(bi + 1) * tm, cs:cs + strip] = acc
            done.append(acc)
        strips.append(jnp.concatenate(done, axis=0))
    half = wd_ref.shape[0] // 2
    y = (x + _dot(after(hid_ref[:, 0:half], strips[-2]), wd_ref[0:half, :])
         + _dot(after(hid_ref[:, half:], strips[-1]), wd_ref[half:, :]))

    cv = cv_ref[...] + dwb_ref[...]
    mu = jnp.mean(cv, axis=-1, keepdims=True)
    xc = cv - mu
    ln = xc * lax.rsqrt(jnp.mean(xc * xc, axis=-1, keepdims=True) + EPS) * lng_ref[...] + lnb_ref[...]
    act = (ln * _sigmoid(ln)).astype(BF16)
    act_ref[...] = act
    y_ref[...] = after(y, act).reshape(bb, tm, d)


def _convmod_mlp(x, g, buf, dw, dwb, lng, lnb, wpw, gain, wu, wd, tm, bb):
    b, l, d = x.shape
    nt = l // tm
    ntiles = (b // bb) * nt
    per = tm // HALO
    conv_tile = lambda s: jnp.minimum(s, ntiles - 1)
    mlp_tile = lambda s: jnp.maximum(s - 1, 0)
    cur = lambda s: (conv_tile(s) // nt, conv_tile(s) % nt, 0)
    prev = lambda s: (conv_tile(s) // nt, jnp.maximum((conv_tile(s) % nt) * per - 1, 0), 0)
    first = lambda s: (conv_tile(s) // nt, 0, 0)
    out = lambda s: (mlp_tile(s) // nt, mlp_tile(s) % nt, 0)
    return pl.pallas_call(
        functools.partial(_convmod_mlp_kernel, nt=nt, ntiles=ntiles),
        grid=(ntiles + 1,),
        in_specs=[pl.BlockSpec((bb, tm, d), out), pl.BlockSpec((bb, tm, d), cur),
                  pl.BlockSpec((bb, HALO, d), prev), pl.BlockSpec((bb, HALO, d), first),
                  _const_spec(dw.shape), _const_spec(dwb.shape), _const_spec(lng.shape),
                  _const_spec(lnb.shape), _const_spec(wpw.shape), _const_spec(gain.shape),
                  _const_spec(wu.shape), _const_spec(wd.shape)],
        out_specs=pl.BlockSpec((bb, tm, d), out),
        out_shape=jax.ShapeDtypeStruct((b, l, d), F32),
        scratch_shapes=[pltpu.VMEM((bb, HALO + tm, d), F32), pltpu.VMEM((bb * tm, d), F32),
                        pltpu.VMEM((bb * tm, d), BF16), pltpu.VMEM((bb * tm, wu.shape[1]), BF16)],
        compiler_params=_params("arbitrary"),
        name="convmod_mlp",
    )(x, g, g, buf, dw, dwb, lng, lnb, wpw, gain, wu, wd)


def _tile(n, pref):
    t = min(n, pref)
    assert n % t == 0
    return t


def _pad_lanes(v, offset=0):
    return jnp.zeros((1, LANES), F32).at[0, offset:offset + v.shape[0]].set(v.astype(F32))


def _trunk(x, past, w):
    b, l, d = x.shape
    t = b * l
    x2 = x.reshape(t, d)
    tm = _tile(t, 512)

    qkv, z, qn, kn, v, gates = _inproj(x2, w["norm_mix_e"], w["w_main"], w["a_log"],
                                       w["dt_bias"], w["b_f"], w["q_norm_b"], w["k_norm_b"], tm)
    r3 = lambda a: a.reshape((b, l) + a.shape[1:])
    if past is None:
        buf_a = jnp.zeros((b, CONV_A - 1, qkv.shape[-1]), F32)
        s0 = jnp.zeros((b, H_A, DK_A, DV_A), F32)
    else:
        buf_a, s0 = past["conv_a"], past["delta"]
    o_a, nbuf_a, s_a = _gdn(r3(qkv), r3(z), r3(gates), buf_a, s0, w["conv_a_w"], w["g_norm_a"])

    if past is None:
        tq = _tile(l, 1024)
        qa, ka, vt = _fox_prep(r3(qn), r3(kn), r3(v), r3(gates), tq)
        o_b = _fox(qa, ka, vt)
    else:
        lfp = jnp.pad(jnp.transpose(past["lf_b"], (0, 2, 1)), ((0, 0), (0, 8 - H_B), (0, 0)))
        lf_new = r3(gates)[:, :, 2 * H_A:2 * H_A + H_B]
        lfn = jnp.pad(jnp.transpose(lf_new, (0, 2, 1)), ((0, 0), (0, 8 - H_B), (0, LANES - l)))
        flat = lambda c: c.reshape(b, c.shape[1], H_B * D_B)
        o_b = _fox_dec(r3(qn), r3(kn), r3(v), flat(past["k_b"]), flat(past["v_b"]), lfp, lfn, r3(gates))

    x2 = _mix_mlp(x2, o_a.reshape(t, -1), o_b.reshape(t, -1), w["w_out"], w["norm_mlp0"],
                  w["w_up0"], w["w_down0"], tm)

    g = _glu(x2, w["norm_mix_o"], w["w_glu"], tm)
    g3 = g.reshape(b, l, d)
    if past is None:
        buf_c = jnp.zeros((b, CONV_C - 1, d), F32)
    else:
        buf_c = past["conv_c"]
    buf_pad = jnp.concatenate([jnp.zeros((b, HALO - (CONV_C - 1), d), F32), buf_c], axis=1)
    tc = _tile(l, 512)
    y = _convmod_mlp(x2.reshape(b, l, d), g3, buf_pad, w["dw_c"], w["dw_c_b"], w["ln_c_g"], w["ln_c_b"],
                     w["w_pw_c"], w["norm_mlp1"], w["w_up1"], w["w_down1"], tc, _tile(b, max(512 // tc, 1)))
    nbuf_c = jnp.concatenate([buf_c, g3], axis=1)[:, l:, :] if l < CONV_C - 1 else g3[:, l - (CONV_C - 1):, :]

    new = {
        "conv_a": nbuf_a[None], "delta": s_a[None],
        "k_b": kn.reshape(1, b, l, H_B, D_B), "v_b": v.reshape(1, b, l, H_B, D_B),
        "lf_b": r3(gates)[None, :, :, 2 * H_A:2 * H_A + H_B], "conv_c": nbuf_c[None],
    }
    return y, new


def kernel(x_prompt, x_sample, state_conv_a, state_delta_a, cache_k_b, cache_v_b, cache_logf_b, state_conv_c,
           norm_mix_e, w_in_e, b_f, conv_a_w, a_log, dt_bias, g_norm_a, q_norm_b, k_norm_b, w_out_e,
           norm_mix_o, w_glu, dw_c, dw_c_b, ln_c_g, ln_c_b, w_pw_c, norm_mlp, w_up, w_down):
    n_qkv = H_A * (2 * DK_A + DV_A)
    n_z = H_A * DV_A
    n_b = H_B * D_B
    w_in = w_in_e[0]
    o_a = n_qkv + n_z
    o_q = o_a + 2 * H_A
    o_f = o_q + 3 * n_b
    w_main = jnp.concatenate([w_in[:, :o_a], w_in[:, o_q:o_f], w_in[:, o_a:o_q], w_in[:, o_f:o_f + H_B],
                              jnp.zeros((w_in.shape[0], LANES - 2 * H_A - H_B), F32)], axis=1).astype(BF16)
    row = lambda a: a.reshape(1, -1).astype(F32)
    w = {
        "norm_mix_e": row(norm_mix_e[0]), "w_main": w_main,
        "a_log": _pad_lanes(a_log[0]), "dt_bias": _pad_lanes(dt_bias[0]),
        "b_f": _pad_lanes(b_f[0], 2 * H_A),
        "q_norm_b": row(q_norm_b[0]), "k_norm_b": row(k_norm_b[0]),
        "conv_a_w": conv_a_w[0], "g_norm_a": row(g_norm_a[0]),
        "w_out": w_out_e[0].astype(BF16),
        "norm_mlp0": row(norm_mlp[0]), "w_up0": w_up[0].astype(BF16), "w_down0": w_down[0].astype(BF16),
        "norm_mix_o": row(norm_mix_o[0]), "w_glu": w_glu[0].astype(BF16),
        "dw_c": dw_c[0], "dw_c_b": row(dw_c_b[0]), "ln_c_g": row(ln_c_g[0]), "ln_c_b": row(ln_c_b[0]),
        "w_pw_c": w_pw_c[0].astype(BF16),
        "norm_mlp1": row(norm_mlp[1]), "w_up1": w_up[1].astype(BF16), "w_down1": w_down[1].astype(BF16),
    }
    y_prompt, ps = _trunk(x_prompt, None, w)
    past = {"conv_a": state_conv_a[0], "delta": state_delta_a[0], "k_b": cache_k_b[0], "v_b": cache_v_b[0],
            "lf_b": cache_logf_b[0], "conv_c": state_conv_c[0]}
    y_sample, ss = _trunk(x_sample, past, w)
    return (y_prompt, y_sample,
            ps["conv_a"], ps["delta"], ps["k_b"], ps["v_b"], ps["lf_b"], ps["conv_c"],
            ss["conv_a"], ss["delta"], ss["k_b"], ss["v_b"], ss["lf_b"], ss["conv_c"])
```

```python
import functools
import math

import jax
import jax.numpy as jnp
from jax import lax
from jax.experimental import pallas as pl
from jax.experimental.pallas import tpu as pltpu

F32 = jnp.float32
BF16 = jnp.bfloat16
EPS = 1e-6
LANES = 128
VMEM_LIMIT = 56 * 1024 * 1024

H_A = 4
DK_A = 128
DV_A = 128
CONV_A = 4
H_B = 4
D_B = 128
CONV_C = 31
CHUNK = 64
QK_AUG = 2 * D_B
LOG2E = 1.4426950408889634


def _params(*sem, flags=None):
    return pltpu.CompilerParams(dimension_semantics=sem, vmem_limit_bytes=VMEM_LIMIT, flags=flags)


def _const_spec(shape):
    nd = len(shape)
    return pl.BlockSpec(shape, lambda *_: (0,) * nd, pipeline_mode=pl.Buffered(1))


def _rms(x, gain):
    return x * lax.rsqrt(jnp.mean(x * x, axis=-1, keepdims=True) + EPS) * gain


def _sigmoid(x):
    return 1.0 / (1.0 + jnp.exp(-x))


def _softplus(x):
    return jnp.maximum(x, 0.0) + jnp.log(1.0 + jnp.exp(-jnp.abs(x)))


def _dot(a, b):
    return jnp.dot(a, b, preferred_element_type=F32)


def _dot_nt(a, b):
    return lax.dot_general(a, b, (((1,), (1,)), ((), ())), preferred_element_type=F32)


def _dot_tn(a, b):
    return lax.dot_general(a, b, (((0,), (0,)), ((), ())), preferred_element_type=F32)


def _split3(x):
    hi = x.astype(BF16)
    r = x - hi.astype(F32)
    mid = r.astype(BF16)
    lo = (r - mid.astype(F32)).astype(BF16)
    return hi, mid, lo


def _after(never, value, witness):
    w = witness.shape[1]
    head = jnp.where(never, witness.astype(value.dtype), value[:, 0:w])
    return head if w == value.shape[1] else jnp.concatenate([head, value[:, w:]], axis=1)


def _cumsum_rows(x):
    n = x.shape[0]
    row = lax.broadcasted_iota(jnp.int32, x.shape, 0)
    s = 1
    while s < n:
        x = x + jnp.where(row >= s, pltpu.roll(x, s, axis=0), 0.0)
        s *= 2
    return x


def _cumsum_lanes(x):
    n = x.shape[1]
    col = lax.broadcasted_iota(jnp.int32, x.shape, 1)
    s = 1
    while s < n:
        x = x + jnp.where(col >= s, pltpu.roll(x, s, axis=1), 0.0)
        s *= 2
    return x


def _inproj_kernel(x_ref, gain_ref, wm_ref, alog_ref, dtb_ref, bf_ref, qg_ref, kg_ref, *refs, nt):
    if nt is None:
        qkv_ref, z_ref, kn_ref, v_ref, gates_ref, qn_ref = refs
    else:
        qkv_ref, z_ref, kn_ref, v_ref, gates_ref, qa_ref, ka_ref, vt_ref, carry = refs
    tm = x_ref.shape[0]
    h = _rms(x_ref[...], gain_ref[...]).astype(BF16)
    n_qkv = qkv_ref.shape[1]
    n_z = z_ref.shape[1]
    n_b = H_B * D_B
    off = n_qkv + n_z
    vg = _dot(h, wm_ref[:, off + 2 * n_b:off + 3 * n_b + LANES])
    q = _dot(h, wm_ref[:, off:off + n_b])
    k = _dot(h, wm_ref[:, off + n_b:off + 2 * n_b])
    z_ref[...] = _dot(h, wm_ref[:, n_qkv:n_qkv + n_z])
    g = vg[:, n_b:n_b + LANES]
    lane = lax.broadcasted_iota(jnp.int32, g.shape, 1)
    lg = -jnp.exp(alog_ref[...]) * _softplus(g + dtb_ref[...])
    beta = _sigmoid(g)
    lf = -_softplus(-(g + bf_ref[...]))
    gates = jnp.where(lane < H_A, lg, jnp.where(lane < 2 * H_A, beta,
                                                jnp.where(lane < 2 * H_A + H_B, lf, 0.0)))
    gates_ref[...] = gates
    if nt is not None:
        @pl.when(pl.program_id(0) % nt == 0)
        def _():
            carry[...] = jnp.zeros_like(carry)

        csum = _cumsum_rows(gates) + carry[...]
        carry[...] = csum[tm - 1:tm, :]
    never = pl.program_id(0) < 0
    piece = n_qkv // (H_B - 1)
    qkv_ref[:, 0:piece] = _dot(h, wm_ref[:, 0:piece])
    h_tied = h
    for hh in range(H_B):
        sl = slice(hh * D_B, (hh + 1) * D_B)
        qn = _rms(q[:, sl], qg_ref[...])
        kn = _rms(k[:, sl], kg_ref[...])
        heads_rows = pl.ds(hh, tm, stride=H_B)
        kn_ref[heads_rows, :] = kn
        v_ref[heads_rows, :] = vg[:, sl]
        if nt is None:
            qn_ref[:, sl] = qn
        else:
            c2 = jnp.broadcast_to(csum[:, 2 * H_A + hh:2 * H_A + hh + 1] * LOG2E, (tm, LANES))
            hi, mid, lo = (piece.astype(F32) for piece in _split3(c2))
            q_aug = jnp.where(lane == 0, hi, jnp.where(lane == 1, mid, jnp.where(lane == 2, lo,
                              jnp.where(lane < 6, 1.0, 0.0)))).astype(BF16)
            k_aug = jnp.where(lane < 3, 1.0, jnp.where(lane == 3, -hi, jnp.where(lane == 4, -mid,
                              jnp.where(lane == 5, -lo, 0.0)))).astype(BF16)
            q_main = (qn * (D_B ** -0.5 * LOG2E)).astype(BF16)
            k_main = kn.astype(BF16)
            qa_ref[0, hh, :, 0:D_B] = q_main
            qa_ref[0, hh, :, D_B:QK_AUG] = q_aug
            ka_ref[0, hh, :, 0:D_B] = k_main
            ka_ref[0, hh, :, D_B:QK_AUG] = k_aug
            vt_ref[0, hh, 0] = vg[:, sl].T.astype(BF16)
            for witness in (q_main, q_aug, k_main, k_aug):
                h_tied = _after(never, h_tied, witness)
        if hh < H_B - 2:
            cols = slice((hh + 1) * piece, (hh + 2) * piece)
            qkv_ref[:, cols] = _dot(h_tied, wm_ref[:, cols])


def _inproj(x, gain, wm, alog, dtb, bfr, qg, kg, tm, fox_block=None):
    b, l, d = x.shape
    t = b * l
    n_qkv = H_A * (2 * DK_A + DV_A)
    n_z = H_A * DV_A
    n_b = H_B * D_B
    row = lambda i: (i, 0)
    outs = [(1, n_qkv), (1, n_z), (H_B, D_B), (H_B, D_B), (1, LANES)]
    out_specs = [pl.BlockSpec((tm * r, n), row) for r, n in outs]
    out_shape = [jax.ShapeDtypeStruct((t * r, n), F32) for r, n in outs]
    scratch = []
    if fox_block is None:
        nt = None
        out_specs.append(pl.BlockSpec((tm, n_b), row))
        out_shape.append(jax.ShapeDtypeStruct((t, n_b), F32))
    else:
        nt = l // tm
        per = fox_block // tm
        aug = pl.BlockSpec((1, H_B, tm, QK_AUG), lambda i: (i // nt, 0, i % nt, 0))
        out_specs += [aug, aug, pl.BlockSpec((1, H_B, 1, D_B, tm),
                                             lambda i: (i // nt, 0, (i % nt) // per, 0, (i % nt) % per))]
        out_shape += [jax.ShapeDtypeStruct((b, H_B, l, QK_AUG), BF16),
                      jax.ShapeDtypeStruct((b, H_B, l, QK_AUG), BF16),
                      jax.ShapeDtypeStruct((b, H_B, l // fox_block, D_B, fox_block), BF16)]
        scratch = [pltpu.VMEM((1, LANES), F32)]
    return pl.pallas_call(
        functools.partial(_inproj_kernel, nt=nt),
        grid=(t // tm,),
        in_specs=[pl.BlockSpec((tm, d), row), _const_spec(gain.shape), _const_spec(wm.shape),
                  _const_spec(alog.shape), _const_spec(dtb.shape),
                  _const_spec(bfr.shape), _const_spec(qg.shape), _const_spec(kg.shape)],
        out_specs=out_specs,
        out_shape=out_shape,
        scratch_shapes=scratch,
        compiler_params=_params("arbitrary"),
        name="inproj",
    )(x.reshape(t, d), gain, wm, alog, dtb, bfr, qg, kg)


def _each(f, *lists):
    return [f(*xs) for xs in zip(*lists)]


def _mm(x, y):
    return _dot(x.astype(BF16), y.astype(BF16))


def _unit_lower_inverse_minus_eye(a_list, same_block):
    d = _each(lambda a: jnp.where(same_block, a, 0.0), a_list)
    nn = _each(lambda a, dd: a - dd, a_list, d)
    d2 = _each(lambda x: _mm(x, x), d)
    d4 = _each(lambda x: _mm(x, x), d2)
    e = _each(lambda dd, x2: x2 - dd - _mm(dd, x2), d, d2)
    d8 = _each(lambda x: _mm(x, x), d4)
    e = _each(lambda ee, x4: ee + x4 + _mm(ee, x4), e, d4)
    e = _each(lambda ee, x8: ee + x8 + _mm(ee, x8), e, d8)
    m = _each(lambda ee, n: n + _mm(ee, n), e, nn)
    m2 = _each(lambda x: _mm(x, x), m)
    ex = _each(lambda ee, mm: ee - mm - _mm(mm, ee), e, m)
    return _each(lambda x, mm2: x + mm2 + _mm(mm2, x), ex, m2)


def _gdn_kernel(qkv_ref, z_ref, gates_ref, buf0_ref, s0_ref, cw_ref, gn_ref,
                o_ref, nbuf_ref, sout_ref, xbuf, state):
    c = pl.program_id(1)
    L = CHUNK
    bb, rows = qkv_ref.shape[0], qkv_ref.shape[1]
    nc = rows // L
    pad = 8
    bf = lambda x: x.astype(BF16)

    @pl.when(c == 0)
    def _():
        xbuf[:, pad - (CONV_A - 1):pad, :] = buf0_ref[...]
        state[...] = s0_ref[...]

    xbuf[:, pad:pad + rows, :] = qkv_ref[...]
    w = cw_ref[...]
    ys = []
    for bi in range(bb):
        y = xbuf[bi, pad:pad + rows, :] * w[CONV_A - 1:CONV_A, :]
        for i in range(1, CONV_A):
            y = y + xbuf[bi, pad - i:pad - i + rows, :] * w[CONV_A - 1 - i:CONV_A - i, :]
        ys.append(y * _sigmoid(y))
    tail = xbuf[:, pad + rows - (CONV_A - 1):pad + rows, :]
    xbuf[:, pad - (CONV_A - 1):pad, :] = tail
    nbuf_ref[...] = tail

    r = lax.broadcasted_iota(jnp.int32, (L, L), 0)
    cc = lax.broadcasted_iota(jnp.int32, (L, L), 1)
    causal = r >= cc
    strict = r > cc
    same_block = (r // 16) == (cc // 16)

    units = [(bi, ci, h) for bi in range(bb) for ci in range(nc) for h in range(H_A)]
    gam = {}
    for bi in range(bb):
        for ci in range(nc):
            gates = gates_ref[bi, ci * L:(ci + 1) * L, :]
            gam_all = _cumsum_rows(gates)
            gam_t = jnp.concatenate([gam_all, jnp.zeros_like(gam_all)], axis=0).T
            gam[bi, ci] = (gates, gam_all, gam_t)

    def normalized(bi, ci, h):
        rs = slice(ci * L, (ci + 1) * L)
        q = ys[bi][rs, h * DK_A:(h + 1) * DK_A]
        k = ys[bi][rs, H_A * DK_A + h * DK_A:H_A * DK_A + (h + 1) * DK_A]
        v = ys[bi][rs, 2 * H_A * DK_A + h * DV_A:2 * H_A * DK_A + (h + 1) * DV_A]
        q = q * lax.rsqrt(jnp.sum(q * q, axis=-1, keepdims=True) + EPS) * (DK_A ** -0.5)
        k = k * lax.rsqrt(jnp.sum(k * k, axis=-1, keepdims=True) + EPS)
        return q, k, v

    qkv_n = [normalized(*u) for u in units]
    q = [t[0] for t in qkv_n]
    k = [t[1] for t in qkv_n]
    v = [t[2] for t in qkv_n]
    gcol = [gam[bi, ci][1][:, h:h + 1] for bi, ci, h in units]
    grow = [gam[bi, ci][2][h:h + 1, 0:L] for bi, ci, h in units]
    beta = [gam[bi, ci][0][:, H_A + h:H_A + h + 1] for bi, ci, h in units]
    g_last = [gam[bi, ci][1][L - 1:L, h:h + 1] for bi, ci, h in units]
    decay = _each(lambda gc, gr: jnp.exp(jnp.where(causal, gc - gr, -jnp.inf)), gcol, grow)
    eg = _each(jnp.exp, gcol)
    kb = _each(bf, k)
    qb = _each(bf, q)
    kk = _each(_dot_nt, kb, kb)
    qk = _each(_dot_nt, qb, kb)
    a_mat = _each(lambda b_, d_, kk_: jnp.where(strict, b_ * d_ * kk_, 0.0), beta, decay, kk)
    p_mat = _each(lambda d_, qk_: bf(d_ * qk_), decay, qk)
    e_inv = _unit_lower_inverse_minus_eye(a_mat, same_block)
    x = _each(lambda b_, v_, eg_, k_: jnp.concatenate([b_ * v_, (b_ * eg_) * k_], axis=1), beta, v, eg, k)
    tx = _each(lambda x_, e_: x_ + _mm(e_, x_), x, e_inv)
    u0 = [t[:, 0:DV_A] for t in tx]
    wq = _each(lambda t, eg_, q_: jnp.concatenate([bf(t[:, DV_A:]), bf(eg_ * q_)], axis=0), tx, eg, q)
    kd = _each(lambda gl, gc, k_: bf(jnp.exp(gl - gc) * k_), g_last, gcol, k)
    dec = _each(jnp.exp, g_last)

    chains = [(bi, h) for bi in range(bb) for h in range(H_A)]
    s = [state[bi, h] for bi, h in chains]
    for ci in range(nc):
        ids = [units.index((bi, ci, h)) for bi, h in chains]
        ws_qs = [_dot(wq[u], bf(s_)) for u, s_ in zip(ids, s)]
        u_b = [bf(u0[u] - t[0:L]) for u, t in zip(ids, ws_qs)]
        pu = [_dot(p_mat[u], ub_) for u, ub_ in zip(ids, u_b)]
        ku = [_dot_tn(kd[u], ub_) for u, ub_ in zip(ids, u_b)]
        s = [dec[u] * s_ + ku_ for u, s_, ku_ in zip(ids, s, ku)]
        for (bi, h), t, pu_ in zip(chains, ws_qs, pu):
            zz = z_ref[bi, ci * L:(ci + 1) * L, h * DV_A:(h + 1) * DV_A]
            o_ref[bi, ci * L:(ci + 1) * L, h * DV_A:(h + 1) * DV_A] = (
                _rms(t[L:2 * L] + pu_, gn_ref[...]) * (zz * _sigmoid(zz)))
    for (bi, h), s_ in zip(chains, s):
        state[bi, h] = s_
        sout_ref[bi, h] = s_


GDN_CHUNKS_PER_STEP = 8


def _gdn(qkv, z, gates, buf0, s0, conv_w, g_norm):
    b, l, n_qkv = qkv.shape
    nc = min(l // CHUNK, 4)
    rows = nc * CHUNK
    bb = _tile(b, max(GDN_CHUNKS_PER_STEP // nc, 1))
    n_z = z.shape[2]
    blk = lambda bi, ci: (bi, ci, 0)
    per_b3 = lambda bi, ci: (bi, 0, 0)
    per_b4 = lambda bi, ci: (bi, 0, 0, 0)
    return pl.pallas_call(
        _gdn_kernel,
        grid=(b // bb, l // rows),
        in_specs=[pl.BlockSpec((bb, rows, n_qkv), blk), pl.BlockSpec((bb, rows, n_z), blk),
                  pl.BlockSpec((bb, rows, LANES), blk),
                  pl.BlockSpec((bb, CONV_A - 1, n_qkv), per_b3),
                  pl.BlockSpec((bb, H_A, DK_A, DV_A), per_b4),
                  _const_spec(conv_w.shape), _const_spec(g_norm.shape)],
        out_specs=[pl.BlockSpec((bb, rows, n_z), blk),
                   pl.BlockSpec((bb, CONV_A - 1, n_qkv), per_b3),
                   pl.BlockSpec((bb, H_A, DK_A, DV_A), per_b4)],
        out_shape=[jax.ShapeDtypeStruct((b, l, n_z), F32),
                   jax.ShapeDtypeStruct((b, CONV_A - 1, n_qkv), F32),
                   jax.ShapeDtypeStruct((b, H_A, DK_A, DV_A), F32)],
        scratch_shapes=[pltpu.VMEM((bb, 8 + rows, n_qkv), F32), pltpu.VMEM((bb, H_A, DK_A, DV_A), F32)],
        compiler_params=_params("parallel", "arbitrary"),
        name="gdn",
    )(qkv, z, gates, buf0, s0, conv_w, g_norm)


FOX_GROUP = 256


def _fox_kernel(qa_ref, ka_ref, vt_ref, o_ref, s_a, s_b, m_s, l_s, acc_s):
    i = pl.program_id(2)
    tq = qa_ref.shape[3]
    ng = tq // FOX_GROUP
    m_s[...] = jnp.full(m_s.shape, -jnp.inf, F32)
    l_s[...] = jnp.zeros(l_s.shape, F32)
    acc_s[...] = jnp.zeros(acc_s.shape, F32)

    def produce(j, s_ref):
        kj = ka_ref[0, 0, j]
        for g in range(ng):
            s_ref[g] = _dot_nt(kj, qa_ref[0, 0, 0, g * FOX_GROUP:(g + 1) * FOX_GROUP, :])

    def consume(j, s_ref, diagonal):
        for g in range(ng):
            rows = (g + 1) * FOX_GROUP if diagonal else s_ref.shape[1]
            s = s_ref[g, 0:rows, :]
            if diagonal:
                kpos = lax.broadcasted_iota(jnp.int32, s.shape, 0)
                qpos = lax.broadcasted_iota(jnp.int32, s.shape, 1) + g * FOX_GROUP
                s = jnp.where(kpos <= qpos, s, -jnp.inf)
            m_old = m_s[g]
            m_new = jnp.maximum(m_old, jnp.max(s, axis=0, keepdims=True))
            p = jnp.exp2(s - m_new)
            corr = jnp.exp2(m_old - m_new)
            l_s[g] = l_s[g] * corr + jnp.sum(p, axis=0, keepdims=True)
            acc_s[g] = acc_s[g] * corr + _dot(vt_ref[0, 0, j, :, 0:rows], p.astype(BF16))
            m_s[g] = m_new

    produce(0, s_a)

    def pair(t, carry):
        j = 2 * t
        produce(j + 1, s_b)
        consume(j, s_a, False)
        produce(j + 2, s_a)
        consume(j + 1, s_b, False)
        return carry

    lax.fori_loop(0, i // 2, pair, 0)

    @pl.when(i % 2 == 0)
    def _():
        consume(i, s_a, True)

    @pl.when(i % 2 == 1)
    def _():
        produce(i, s_b)
        consume(i - 1, s_a, False)
        consume(i, s_b, True)

    for g in range(ng):
        o_ref[0, g * FOX_GROUP:(g + 1) * FOX_GROUP, :] = (acc_s[g] / l_s[g]).T


def _fox(qa, ka, vt):
    b, hb, nb, tq, _ = qa.shape
    ng = tq // FOX_GROUP
    return pl.pallas_call(
        _fox_kernel,
        grid=(b, hb, nb),
        in_specs=[pl.BlockSpec((1, 1, 1, tq, QK_AUG), lambda bi, h, i: (bi, h, i, 0, 0)),
                  pl.BlockSpec((1, 1, nb, tq, QK_AUG), lambda bi, h, i: (bi, h, 0, 0, 0)),
                  pl.BlockSpec((1, 1, nb, D_B, tq), lambda bi, h, i: (bi, h, 0, 0, 0))],
        out_specs=pl.BlockSpec((1, tq, D_B), lambda bi, h, i: (bi, i, h)),
        out_shape=jax.ShapeDtypeStruct((b, nb * tq, hb * D_B), F32),
        scratch_shapes=[pltpu.VMEM((ng, tq, FOX_GROUP), F32), pltpu.VMEM((ng, tq, FOX_GROUP), F32),
                        pltpu.VMEM((ng, 1, FOX_GROUP), F32), pltpu.VMEM((ng, 1, FOX_GROUP), F32),
                        pltpu.VMEM((ng, D_B, FOX_GROUP), F32)],
        compiler_params=_params("parallel", "parallel", "arbitrary"),
        name="fox",
    )(qa, ka, vt)


def _fox_dec_kernel(q_ref, kn_ref, vn_ref, kp_ref, vp_ref, lfp_ref, lfn_ref, gates_ref, o_ref):
    L = q_ref.shape[1]
    t_past = lfp_ref.shape[2]
    bf = lambda x: x.astype(BF16)
    heads = range(H_B)
    c_past = _cumsum_lanes(lfp_ref[0])
    total = c_past[:, t_past - 1:t_past]
    c_new = total + _cumsum_lanes(lfn_ref[0])
    c_col = _cumsum_rows(gates_ref[0])
    c_q = [total[h:h + 1, :] + c_col[:, 2 * H_A + h:2 * H_A + h + 1] for h in heads]

    scale = D_B ** -0.5
    r = lax.broadcasted_iota(jnp.int32, (L, L), 0)
    cc = lax.broadcasted_iota(jnp.int32, (L, L), 1)
    hs = [slice(h * D_B, (h + 1) * D_B) for h in heads]
    q = [bf(q_ref[0, :, hs[h]]) for h in heads]
    past = [pl.ds(h, t_past, stride=H_B) for h in heads]
    s_past = [_dot_nt(q[h], bf(kp_ref[0, past[h], :])) for h in heads]
    new = [pl.ds(h, L, stride=H_B) for h in heads]
    s_new = [_dot_nt(q[h], bf(kn_ref[0, new[h], :])) for h in heads]
    s_past = [s_past[h] * scale + c_q[h] - c_past[h:h + 1, :] for h in heads]
    s_new = [jnp.where(cc <= r, s_new[h] * scale + c_q[h] - c_new[h:h + 1, 0:L], -jnp.inf) for h in heads]
    m = [jnp.maximum(jnp.max(s_past[h], axis=1, keepdims=True), jnp.max(s_new[h], axis=1, keepdims=True))
         for h in heads]
    p_past = [jnp.exp(s_past[h] - m[h]) for h in heads]
    p_new = [jnp.exp(s_new[h] - m[h]) for h in heads]
    denom = [jnp.sum(p_past[h], axis=1, keepdims=True) + jnp.sum(p_new[h], axis=1, keepdims=True) for h in heads]
    o = [_dot(bf(p_past[h]), bf(vp_ref[0, past[h], :])) + _dot(bf(p_new[h]), bf(vn_ref[0, new[h], :])) for h in heads]
    for h in heads:
        o_ref[0, :, hs[h]] = o[h] / denom[h]


def _fox_dec(qn, kn, v, k_past, v_past, lfp, lfn, gates):
    b, l, n_b = qn.shape
    rows_past = k_past.shape[1]
    t_past = rows_past // H_B
    per3 = lambda bi: (bi, 0, 0)
    return pl.pallas_call(
        _fox_dec_kernel,
        grid=(b,),
        in_specs=[pl.BlockSpec((1, l, n_b), per3), pl.BlockSpec((1, l * H_B, D_B), per3),
                  pl.BlockSpec((1, l * H_B, D_B), per3),
                  pl.BlockSpec((1, rows_past, D_B), per3), pl.BlockSpec((1, rows_past, D_B), per3),
                  pl.BlockSpec((1, 8, t_past), per3), pl.BlockSpec((1, 8, LANES), per3),
                  pl.BlockSpec((1, l, LANES), per3)],
        out_specs=pl.BlockSpec((1, l, n_b), per3),
        out_shape=jax.ShapeDtypeStruct((b, l, n_b), F32),
        compiler_params=_params("parallel"),
        name="fox_dec",
    )(qn, kn, v, k_past, v_past, lfp, lfn, gates)


def _mlp(x, gain_ref, wu_ref, wd_ref, hid_ref):
    hn = _rms(x, gain_ref[...]).astype(BF16)
    d_ff = wu_ref.shape[1]
    ck = 1024
    for c in range(d_ff // ck):
        a = jnp.maximum(_dot(hn, wu_ref[:, c * ck:(c + 1) * ck]), 0.0)
        hid_ref[:, c * ck:(c + 1) * ck] = (a * a).astype(BF16)
    return x + _dot(hid_ref[...], wd_ref[...])


def _mix_mlp_kernel(x_ref, oa_ref, ob_ref, wo_ref, gain_ref, wu_ref, wd_ref, y_ref, hid_ref):
    n_a = oa_ref.shape[1]
    x = (x_ref[...] + _dot(oa_ref[...].astype(BF16), wo_ref[0:n_a, :])
         + _dot(ob_ref[...].astype(BF16), wo_ref[n_a:, :]))
    y_ref[...] = _mlp(x, gain_ref, wu_ref, wd_ref, hid_ref)


def _mix_mlp(x, oa, ob, wo, gain, wu, wd, tm):
    t, d = x.shape
    row = lambda i: (i, 0)
    return pl.pallas_call(
        _mix_mlp_kernel,
        grid=(t // tm,),
        in_specs=[pl.BlockSpec((tm, d), row), pl.BlockSpec((tm, oa.shape[1]), row),
                  pl.BlockSpec((tm, ob.shape[1]), row), _const_spec(wo.shape),
                  _const_spec(gain.shape), _const_spec(wu.shape), _const_spec(wd.shape)],
        out_specs=pl.BlockSpec((tm, d), row),
        out_shape=jax.ShapeDtypeStruct((t, d), F32),
        scratch_shapes=[pltpu.VMEM((tm, wu.shape[1]), BF16)],
        compiler_params=_params("parallel"),
        name="mix_mlp",
    )(x, oa, ob, wo, gain, wu, wd)


def _glu_kernel(x_ref, gain_ref, w_ref, g_ref):
    h = _rms(x_ref[...], gain_ref[...]).astype(BF16)
    d = g_ref.shape[1]
    g_ref[...] = _dot(h, w_ref[:, 0:d]) * _sigmoid(_dot(h, w_ref[:, d:2 * d]))


def _glu(x, gain, w, tm):
    t, d = x.shape
    row = lambda i: (i, 0)
    return pl.pallas_call(
        _glu_kernel,
        grid=(t // tm,),
        in_specs=[pl.BlockSpec((tm, d), row), _const_spec(gain.shape), _const_spec(w.shape)],
        out_specs=pl.BlockSpec((tm, d), row),
        out_shape=jax.ShapeDtypeStruct((t, d), F32),
        compiler_params=_params("parallel"),
        name="glu",
    )(x, gain, w)


HALO = 32


def _convmod_mlp_kernel(x_ref, g_ref, prev_ref, buf_ref, dw_ref, dwb_ref, lng_ref, lnb_ref, wpw_ref,
                        gain_ref, wu_ref, wd_ref, y_ref, xp_ref, cv_ref, act_ref, hid_ref, *, nt, ntiles):
    step = pl.program_id(0)
    i = jnp.minimum(step, ntiles - 1) % nt
    bb, tm, d = g_ref.shape
    rows = bb * tm

    @pl.when(step == 0)
    def _():
        act_ref[...] = jnp.zeros_like(act_ref)

    x = x_ref[...].reshape(rows, d) + _dot(act_ref[...], wpw_ref[...])
    hn = _rms(x, gain_ref[...]).astype(BF16)

    xp_ref[:, 0:HALO, :] = jnp.where(i == 0, buf_ref[...], prev_ref[...])
    xp_ref[:, HALO:HALO + tm, :] = g_ref[...]
    n = HALO + tm
    first = HALO - (CONV_C - 1)
    n_strips = wu_ref.shape[1] // 1024
    strip = d // n_strips
    after = functools.partial(_after, step < 0)

    strips = []
    for c in range(n_strips):
        cs = c * strip
        hn_c = hn if c < 2 else after(hn, strips[c - 2])
        up = jnp.maximum(_dot(hn_c, wu_ref[:, c * 1024:(c + 1) * 1024]), 0.0)
        hid_ref[:, c * 1024:(c + 1) * 1024] = (up * up).astype(BF16)
        done = []
        for bi in range(bb):
            xp = xp_ref[bi, :, cs:cs + strip]
            acc = None
            for r in range(8):
                xr = xp if r == 0 else pltpu.roll(xp, n - r, axis=0)
                for a in range((first + CONV_C - 1) // 8 + 1):
                    j = 8 * a + r - first
                    if 0 <= j < CONV_C:
                        term = xr[8 * a:8 * a + tm, :] * dw_ref[j:j + 1, cs:cs + strip]
                        acc = term if acc is None else acc + term
            cv_ref[bi * tm:(bi + 1) * tm, cs:cs + strip] = acc
            done.append(acc)
        strips.append(jnp.concatenate(done, axis=0))
    half = wd_ref.shape[0] // 2
    y = (x + _dot(after(hid_ref[:, 0:half], strips[-2]), wd_ref[0:half, :])
         + _dot(after(hid_ref[:, half:], strips[-1]), wd_ref[half:, :]))

    cv = cv_ref[...] + dwb_ref[...]
    mu = jnp.mean(cv, axis=-1, keepdims=True)
    xc = cv - mu
    ln = xc * lax.rsqrt(jnp.mean(xc * xc, axis=-1, keepdims=True) + EPS) * lng_ref[...] + lnb_ref[...]
    act = (ln * _sigmoid(ln)).astype(BF16)
    act_ref[...] = act
    y_ref[...] = after(y, act).reshape(bb, tm, d)


def _convmod_mlp(x, g, buf, dw, dwb, lng, lnb, wpw, gain, wu, wd, tm, bb):
    b, l, d = x.shape
    nt = l // tm
    ntiles = (b // bb) * nt
    per = tm // HALO
    conv_tile = lambda s: jnp.minimum(s, ntiles - 1)
    mlp_tile = lambda s: jnp.maximum(s - 1, 0)
    cur = lambda s: (conv_tile(s) // nt, conv_tile(s) % nt, 0)
    prev = lambda s: (conv_tile(s) // nt, jnp.maximum((conv_tile(s) % nt) * per - 1, 0), 0)
    first = lambda s: (conv_tile(s) // nt, 0, 0)
    out = lambda s: (mlp_tile(s) // nt, mlp_tile(s) % nt, 0)
    return pl.pallas_call(
        functools.partial(_convmod_mlp_kernel, nt=nt, ntiles=ntiles),
        grid=(ntiles + 1,),
        in_specs=[pl.BlockSpec((bb, tm, d), out), pl.BlockSpec((bb, tm, d), cur),
                  pl.BlockSpec((bb, HALO, d), prev), pl.BlockSpec((bb, HALO, d), first),
                  _const_spec(dw.shape), _const_spec(dwb.shape), _const_spec(lng.shape),
                  _const_spec(lnb.shape), _const_spec(wpw.shape), _const_spec(gain.shape),
                  _const_spec(wu.shape), _const_spec(wd.shape)],
        out_specs=pl.BlockSpec((bb, tm, d), out),
        out_shape=jax.ShapeDtypeStruct((b, l, d), F32),
        scratch_shapes=[pltpu.VMEM((bb, HALO + tm, d), F32), pltpu.VMEM((bb * tm, d), F32),
                        pltpu.VMEM((bb * tm, d), BF16), pltpu.VMEM((bb * tm, wu.shape[1]), BF16)],
        compiler_params=_params("arbitrary"),
        name="convmod_mlp",
    )(x, g, g, buf, dw, dwb, lng, lnb, wpw, gain, wu, wd)


def _tile(n, pref):
    t = min(n, pref)
    assert n % t == 0
    return t


def _pad_lanes(v, offset=0):
    return jnp.zeros((1, LANES), F32).at[0, offset:offset + v.shape[0]].set(v.astype(F32))


def _trunk(x, past, w):
    b, l, d = x.shape
    t = b * l
    x2 = x.reshape(t, d)
    tm = _tile(t, 512)

    tq = _tile(l, 1024) if past is None else None
    qkv, z, kn, v, gates, *q_side = _inproj(x, w["norm_mix_e"], w["w_main"], w["a_log"], w["dt_bias"],
                                            w["b_f"], w["q_norm_b"], w["k_norm_b"], tm, tq)
    r3 = lambda a: a.reshape((b, a.shape[0] // b) + a.shape[1:])
    if past is None:
        buf_a = jnp.zeros((b, CONV_A - 1, qkv.shape[-1]), F32)
        s0 = jnp.zeros((b, H_A, DK_A, DV_A), F32)
    else:
        buf_a, s0 = past["conv_a"], past["delta"]
    o_a, nbuf_a, s_a = _gdn(r3(qkv), r3(z), r3(gates), buf_a, s0, w["conv_a_w"], w["g_norm_a"])

    if past is None:
        qa, ka, vt = q_side
        blocks = lambda a: a.reshape(b, H_B, l // tq, tq, QK_AUG)
        o_b = _fox(blocks(qa), blocks(ka), vt)
    else:
        qn, = q_side
        lfp = jnp.pad(jnp.transpose(past["lf_b"], (0, 2, 1)), ((0, 0), (0, 8 - H_B), (0, 0)))
        lf_new = r3(gates)[:, :, 2 * H_A:2 * H_A + H_B]
        lfn = jnp.pad(jnp.transpose(lf_new, (0, 2, 1)), ((0, 0), (0, 8 - H_B), (0, LANES - l)))
        flat = lambda c: c.reshape(b, c.shape[1] * H_B, D_B)
        o_b = _fox_dec(r3(qn), r3(kn), r3(v), flat(past["k_b"]), flat(past["v_b"]), lfp, lfn, r3(gates))

    x2 = _mix_mlp(x2, o_a.reshape(t, -1), o_b.reshape(t, -1), w["w_out"], w["norm_mlp0"],
                  w["w_up0"], w["w_down0"], tm)

    g = _glu(x2, w["norm_mix_o"], w["w_glu"], tm)
    g3 = g.reshape(b, l, d)
    if past is None:
        buf_c = jnp.zeros((b, CONV_C - 1, d), F32)
    else:
        buf_c = past["conv_c"]
    buf_pad = jnp.concatenate([jnp.zeros((b, HALO - (CONV_C - 1), d), F32), buf_c], axis=1)
    tc = _tile(l, 512)
    y = _convmod_mlp(x2.reshape(b, l, d), g3, buf_pad, w["dw_c"], w["dw_c_b"], w["ln_c_g"], w["ln_c_b"],
                     w["w_pw_c"], w["norm_mlp1"], w["w_up1"], w["w_down1"], tc, _tile(b, max(512 // tc, 1)))
    nbuf_c = jnp.concatenate([buf_c, g3], axis=1)[:, l:, :] if l < CONV_C - 1 else g3[:, l - (CONV_C - 1):, :]

    new = {
        "conv_a": nbuf_a[None], "delta": s_a[None],
        "k_b": kn.reshape(1, b, l, H_B, D_B), "v_b": v.reshape(1, b, l, H_B, D_B),
        "lf_b": r3(gates)[None, :, :, 2 * H_A:2 * H_A + H_B], "conv_c": nbuf_c[None],
    }
    return y, new


def kernel(x_prompt, x_sample, state_conv_a, state_delta_a, cache_k_b, cache_v_b, cache_logf_b, state_conv_c,
           norm_mix_e, w_in_e, b_f, conv_a_w, a_log, dt_bias, g_norm_a, q_norm_b, k_norm_b, w_out_e,
           norm_mix_o, w_glu, dw_c, dw_c_b, ln_c_g, ln_c_b, w_pw_c, norm_mlp, w_up, w_down):
    n_qkv = H_A * (2 * DK_A + DV_A)
    n_z = H_A * DV_A
    n_b = H_B * D_B
    w_in = w_in_e[0]
    o_a = n_qkv + n_z
    o_q = o_a + 2 * H_A
    o_f = o_q + 3 * n_b
    w_main = jnp.concatenate([w_in[:, :o_a], w_in[:, o_q:o_f], w_in[:, o_a:o_q], w_in[:, o_f:o_f + H_B],
                              jnp.zeros((w_in.shape[0], LANES - 2 * H_A - H_B), F32)], axis=1).astype(BF16)
    row = lambda a: a.reshape(1, -1).astype(F32)
    w = {
        "norm_mix_e": row(norm_mix_e[0]), "w_main": w_main,
        "a_log": _pad_lanes(a_log[0]), "dt_bias": _pad_lanes(dt_bias[0]),
        "b_f": _pad_lanes(b_f[0], 2 * H_A),
        "q_norm_b": row(q_norm_b[0]), "k_norm_b": row(k_norm_b[0]),
        "conv_a_w": conv_a_w[0], "g_norm_a": row(g_norm_a[0]),
        "w_out": w_out_e[0].astype(BF16),
        "norm_mlp0": row(norm_mlp[0]), "w_up0": w_up[0].astype(BF16), "w_down0": w_down[0].astype(BF16),
        "norm_mix_o": row(norm_mix_o[0]), "w_glu": w_glu[0].astype(BF16),
        "dw_c": dw_c[0], "dw_c_b": row(dw_c_b[0]), "ln_c_g": row(ln_c_g[0]), "ln_c_b": row(ln_c_b[0]),
        "w_pw_c": w_pw_c[0].astype(BF16),
        "norm_mlp1": row(norm_mlp[1]), "w_up1": w_up[1].astype(BF16), "w_down1": w_down[1].astype(BF16),
    }
    y_prompt, ps = _trunk(x_prompt, None, w)
    past = {"conv_a": state_conv_a[0], "delta": state_delta_a[0], "k_b": cache_k_b[0], "v_b": cache_v_b[0],
            "lf_b": cache_logf_b[0], "conv_c": state_conv_c[0]}
    y_sample, ss = _trunk(x_sample, past, w)
    return (y_prompt, y_sample,
            ps["conv_a"], ps["delta"], ps["k_b"], ps["v_b"], ps["lf_b"], ps["conv_c"],
            ss["conv_a"], ss["delta"], ss["k_b"], ss["v_b"], ss["lf_b"], ss["conv_c"])
```

```python
import functools
import math

import jax
import jax.numpy as jnp
from jax import lax
from jax.experimental import pallas as pl
from jax.experimental.pallas import tpu as pltpu

F32 = jnp.float32
BF16 = jnp.bfloat16
EPS = 1e-6
LANES = 128
VMEM_LIMIT = 56 * 1024 * 1024

H_A = 4
DK_A = 128
DV_A = 128
CONV_A = 4
H_B = 4
D_B = 128
CONV_C = 31
CHUNK = 64
QK_AUG = 2 * D_B
LOG2E = 1.4426950408889634


def _params(*sem, flags=None):
    return pltpu.CompilerParams(dimension_semantics=sem, vmem_limit_bytes=VMEM_LIMIT, flags=flags)


def _const_spec(shape):
    nd = len(shape)
    return pl.BlockSpec(shape, lambda *_: (0,) * nd, pipeline_mode=pl.Buffered(1))


def _rms(x, gain):
    return x * lax.rsqrt(jnp.mean(x * x, axis=-1, keepdims=True) + EPS) * gain


def _sigmoid(x):
    return 1.0 / (1.0 + jnp.exp(-x))


def _softplus(x):
    return jnp.maximum(x, 0.0) + jnp.log(1.0 + jnp.exp(-jnp.abs(x)))


def _dot(a, b):
    return jnp.dot(a, b, preferred_element_type=F32)


def _dot_nt(a, b):
    return lax.dot_general(a, b, (((1,), (1,)), ((), ())), preferred_element_type=F32)


def _dot_tn(a, b):
    return lax.dot_general(a, b, (((0,), (0,)), ((), ())), preferred_element_type=F32)


def _split3(x):
    hi = x.astype(BF16)
    r = x - hi.astype(F32)
    mid = r.astype(BF16)
    lo = (r - mid.astype(F32)).astype(BF16)
    return hi, mid, lo


def _after(never, value, witness):
    w = witness.shape[1]
    head = jnp.where(never, witness.astype(value.dtype), value[:, 0:w])
    return head if w == value.shape[1] else jnp.concatenate([head, value[:, w:]], axis=1)


def _cumsum_rows(x):
    n = x.shape[0]
    row = lax.broadcasted_iota(jnp.int32, x.shape, 0)
    s = 1
    while s < n:
        x = x + jnp.where(row >= s, pltpu.roll(x, s, axis=0), 0.0)
        s *= 2
    return x


def _cumsum_lanes(x):
    n = x.shape[1]
    col = lax.broadcasted_iota(jnp.int32, x.shape, 1)
    s = 1
    while s < n:
        x = x + jnp.where(col >= s, pltpu.roll(x, s, axis=1), 0.0)
        s *= 2
    return x


def _inproj_kernel(x_ref, gain_ref, wm_ref, alog_ref, dtb_ref, bf_ref, qg_ref, kg_ref, *refs, nt):
    if nt is None:
        qkv_ref, z_ref, kn_ref, v_ref, gates_ref, qn_ref = refs
    else:
        qkv_ref, z_ref, kn_ref, v_ref, gates_ref, qa_ref, ka_ref, vt_ref, carry = refs
    tm = x_ref.shape[0]
    h = _rms(x_ref[...], gain_ref[...]).astype(BF16)
    n_qkv = qkv_ref.shape[1]
    n_z = z_ref.shape[1]
    n_b = H_B * D_B
    off = n_qkv + n_z
    vg = _dot(h, wm_ref[:, off + 2 * n_b:off + 3 * n_b + LANES])
    q = _dot(h, wm_ref[:, off:off + n_b])
    k = _dot(h, wm_ref[:, off + n_b:off + 2 * n_b])
    z_ref[...] = _dot(h, wm_ref[:, n_qkv:n_qkv + n_z])
    g = vg[:, n_b:n_b + LANES]
    lane = lax.broadcasted_iota(jnp.int32, g.shape, 1)
    lg = -jnp.exp(alog_ref[...]) * _softplus(g + dtb_ref[...])
    beta = _sigmoid(g)
    lf = -_softplus(-(g + bf_ref[...]))
    gates = jnp.where(lane < H_A, lg, jnp.where(lane < 2 * H_A, beta,
                                                jnp.where(lane < 2 * H_A + H_B, lf, 0.0)))
    gates_ref[...] = gates
    if nt is not None:
        @pl.when(pl.program_id(0) % nt == 0)
        def _():
            carry[...] = jnp.zeros_like(carry)

        csum = _cumsum_rows(gates) + carry[...]
        carry[...] = csum[tm - 1:tm, :]
    never = pl.program_id(0) < 0
    piece = n_qkv // (H_B - 1)
    qkv_ref[:, 0:piece] = _dot(h, wm_ref[:, 0:piece])
    h_tied = h
    for hh in range(H_B):
        sl = slice(hh * D_B, (hh + 1) * D_B)
        qn = _rms(q[:, sl], qg_ref[...])
        kn = _rms(k[:, sl], kg_ref[...])
        heads_rows = pl.ds(hh, tm, stride=H_B)
        kn_ref[heads_rows, :] = kn
        v_ref[heads_rows, :] = vg[:, sl]
        if nt is None:
            qn_ref[:, sl] = qn
        else:
            c2 = jnp.broadcast_to(csum[:, 2 * H_A + hh:2 * H_A + hh + 1] * LOG2E, (tm, LANES))
            hi, mid, lo = (piece.astype(F32) for piece in _split3(c2))
            q_aug = jnp.where(lane == 0, hi, jnp.where(lane == 1, mid, jnp.where(lane == 2, lo,
                              jnp.where(lane < 6, 1.0, 0.0)))).astype(BF16)
            k_aug = jnp.where(lane < 3, 1.0, jnp.where(lane == 3, -hi, jnp.where(lane == 4, -mid,
                              jnp.where(lane == 5, -lo, 0.0)))).astype(BF16)
            q_main = (qn * (D_B ** -0.5 * LOG2E)).astype(BF16)
            k_main = kn.astype(BF16)
            qa_ref[0, hh, :, 0:D_B] = q_main
            qa_ref[0, hh, :, D_B:QK_AUG] = q_aug
            ka_ref[0, hh, :, 0:D_B] = k_main
            ka_ref[0, hh, :, D_B:QK_AUG] = k_aug
            vt_ref[0, hh, 0] = vg[:, sl].T.astype(BF16)
            for witness in (q_main, q_aug, k_main, k_aug):
                h_tied = _after(never, h_tied, witness)
        if hh < H_B - 2:
            cols = slice((hh + 1) * piece, (hh + 2) * piece)
            qkv_ref[:, cols] = _dot(h_tied, wm_ref[:, cols])


def _inproj(x, gain, wm, alog, dtb, bfr, qg, kg, tm, fox_block=None):
    b, l, d = x.shape
    t = b * l
    n_qkv = H_A * (2 * DK_A + DV_A)
    n_z = H_A * DV_A
    n_b = H_B * D_B
    row = lambda i: (i, 0)
    outs = [(1, n_qkv), (1, n_z), (H_B, D_B), (H_B, D_B), (1, LANES)]
    out_specs = [pl.BlockSpec((tm * r, n), row) for r, n in outs]
    out_shape = [jax.ShapeDtypeStruct((t * r, n), F32) for r, n in outs]
    scratch = []
    if fox_block is None:
        nt = None
        out_specs.append(pl.BlockSpec((tm, n_b), row))
        out_shape.append(jax.ShapeDtypeStruct((t, n_b), F32))
    else:
        nt = l // tm
        per = fox_block // tm
        aug = pl.BlockSpec((1, H_B, tm, QK_AUG), lambda i: (i // nt, 0, i % nt, 0))
        out_specs += [aug, aug, pl.BlockSpec((1, H_B, 1, D_B, tm),
                                             lambda i: (i // nt, 0, (i % nt) // per, 0, (i % nt) % per))]
        out_shape += [jax.ShapeDtypeStruct((b, H_B, l, QK_AUG), BF16),
                      jax.ShapeDtypeStruct((b, H_B, l, QK_AUG), BF16),
                      jax.ShapeDtypeStruct((b, H_B, l // fox_block, D_B, fox_block), BF16)]
        scratch = [pltpu.VMEM((1, LANES), F32)]
    return pl.pallas_call(
        functools.partial(_inproj_kernel, nt=nt),
        grid=(t // tm,),
        in_specs=[pl.BlockSpec((tm, d), row), _const_spec(gain.shape), _const_spec(wm.shape),
                  _const_spec(alog.shape), _const_spec(dtb.shape),
                  _const_spec(bfr.shape), _const_spec(qg.shape), _const_spec(kg.shape)],
        out_specs=out_specs,
        out_shape=out_shape,
        scratch_shapes=scratch,
        compiler_params=_params("arbitrary"),
        name="inproj",
    )(x.reshape(t, d), gain, wm, alog, dtb, bfr, qg, kg)


def _each(f, *lists):
    return [f(*xs) for xs in zip(*lists)]


def _mm(x, y):
    return _dot(x.astype(BF16), y.astype(BF16))


def _unit_lower_inverse_minus_eye(a_list, same_block):
    d = _each(lambda a: jnp.where(same_block, a, 0.0), a_list)
    nn = _each(lambda a, dd: a - dd, a_list, d)
    d2 = _each(lambda x: _mm(x, x), d)
    d4 = _each(lambda x: _mm(x, x), d2)
    e = _each(lambda dd, x2: x2 - dd - _mm(dd, x2), d, d2)
    d8 = _each(lambda x: _mm(x, x), d4)
    e = _each(lambda ee, x4: ee + x4 + _mm(ee, x4), e, d4)
    e = _each(lambda ee, x8: ee + x8 + _mm(ee, x8), e, d8)
    m = _each(lambda ee, n: n + _mm(ee, n), e, nn)
    m2 = _each(lambda x: _mm(x, x), m)
    ex = _each(lambda ee, mm: ee - mm - _mm(mm, ee), e, m)
    return _each(lambda x, mm2: x + mm2 + _mm(mm2, x), ex, m2)


def _gdn_kernel(qkv_ref, z_ref, gates_ref, buf0_ref, s0_ref, cw_ref, gn_ref,
                o_ref, nbuf_ref, sout_ref, xbuf, state):
    c = pl.program_id(1)
    L = CHUNK
    bb, rows = qkv_ref.shape[0], qkv_ref.shape[1]
    nc = rows // L
    pad = 8
    bf = lambda x: x.astype(BF16)

    @pl.when(c == 0)
    def _():
        xbuf[:, pad - (CONV_A - 1):pad, :] = buf0_ref[...]
        state[...] = s0_ref[...]

    xbuf[:, pad:pad + rows, :] = qkv_ref[...]
    w = cw_ref[...]
    ys = []
    for bi in range(bb):
        y = xbuf[bi, pad:pad + rows, :] * w[CONV_A - 1:CONV_A, :]
        for i in range(1, CONV_A):
            y = y + xbuf[bi, pad - i:pad - i + rows, :] * w[CONV_A - 1 - i:CONV_A - i, :]
        ys.append(y * _sigmoid(y))
    tail = xbuf[:, pad + rows - (CONV_A - 1):pad + rows, :]
    xbuf[:, pad - (CONV_A - 1):pad, :] = tail
    nbuf_ref[...] = tail

    r = lax.broadcasted_iota(jnp.int32, (L, L), 0)
    cc = lax.broadcasted_iota(jnp.int32, (L, L), 1)
    causal = r >= cc
    strict = r > cc
    same_block = (r // 16) == (cc // 16)

    units = [(bi, ci, h) for bi in range(bb) for ci in range(nc) for h in range(H_A)]
    gam = {}
    for bi in range(bb):
        for ci in range(nc):
            gates = gates_ref[bi, ci * L:(ci + 1) * L, :]
            gam_all = _cumsum_rows(gates)
            gam_t = jnp.concatenate([gam_all, jnp.zeros_like(gam_all)], axis=0).T
            gam[bi, ci] = (gates, gam_all, gam_t)

    def normalized(bi, ci, h):
        rs = slice(ci * L, (ci + 1) * L)
        q = ys[bi][rs, h * DK_A:(h + 1) * DK_A]
        k = ys[bi][rs, H_A * DK_A + h * DK_A:H_A * DK_A + (h + 1) * DK_A]
        v = ys[bi][rs, 2 * H_A * DK_A + h * DV_A:2 * H_A * DK_A + (h + 1) * DV_A]
        q = q * lax.rsqrt(jnp.sum(q * q, axis=-1, keepdims=True) + EPS) * (DK_A ** -0.5)
        k = k * lax.rsqrt(jnp.sum(k * k, axis=-1, keepdims=True) + EPS)
        return q, k, v

    qkv_n = [normalized(*u) for u in units]
    q = [t[0] for t in qkv_n]
    k = [t[1] for t in qkv_n]
    v = [t[2] for t in qkv_n]
    gcol = [gam[bi, ci][1][:, h:h + 1] for bi, ci, h in units]
    grow = [gam[bi, ci][2][h:h + 1, 0:L] for bi, ci, h in units]
    beta = [gam[bi, ci][0][:, H_A + h:H_A + h + 1] for bi, ci, h in units]
    g_last = [gam[bi, ci][1][L - 1:L, h:h + 1] for bi, ci, h in units]
    decay = _each(lambda gc, gr: jnp.exp(jnp.where(causal, gc - gr, -jnp.inf)), gcol, grow)
    eg = _each(jnp.exp, gcol)
    kb = _each(bf, k)
    qb = _each(bf, q)
    kk = _each(_dot_nt, kb, kb)
    qk = _each(_dot_nt, qb, kb)
    a_mat = _each(lambda b_, d_, kk_: jnp.where(strict, b_ * d_ * kk_, 0.0), beta, decay, kk)
    p_mat = _each(lambda d_, qk_: bf(d_ * qk_), decay, qk)
    e_inv = _unit_lower_inverse_minus_eye(a_mat, same_block)
    x = _each(lambda b_, v_, eg_, k_: jnp.concatenate([b_ * v_, (b_ * eg_) * k_], axis=1), beta, v, eg, k)
    tx = _each(lambda x_, e_: x_ + _mm(e_, x_), x, e_inv)
    u0 = [t[:, 0:DV_A] for t in tx]
    wq = _each(lambda t, eg_, q_: jnp.concatenate([bf(t[:, DV_A:]), bf(eg_ * q_)], axis=0), tx, eg, q)
    kd = _each(lambda gl, gc, k_: bf(jnp.exp(gl - gc) * k_), g_last, gcol, k)
    dec = _each(jnp.exp, g_last)

    chains = [(bi, h) for bi in range(bb) for h in range(H_A)]
    s = [state[bi, h] for bi, h in chains]
    for ci in range(nc):
        ids = [units.index((bi, ci, h)) for bi, h in chains]
        ws_qs = [_dot(wq[u], bf(s_)) for u, s_ in zip(ids, s)]
        u_b = [bf(u0[u] - t[0:L]) for u, t in zip(ids, ws_qs)]
        pu = [_dot(p_mat[u], ub_) for u, ub_ in zip(ids, u_b)]
        ku = [_dot_tn(kd[u], ub_) for u, ub_ in zip(ids, u_b)]
        s = [dec[u] * s_ + ku_ for u, s_, ku_ in zip(ids, s, ku)]
        for (bi, h), t, pu_ in zip(chains, ws_qs, pu):
            zz = z_ref[bi, ci * L:(ci + 1) * L, h * DV_A:(h + 1) * DV_A]
            o_ref[bi, ci * L:(ci + 1) * L, h * DV_A:(h + 1) * DV_A] = (
                _rms(t[L:2 * L] + pu_, gn_ref[...]) * (zz * _sigmoid(zz)))
    for (bi, h), s_ in zip(chains, s):
        state[bi, h] = s_
        sout_ref[bi, h] = s_


GDN_CHUNKS_PER_STEP = 8


def _gdn(qkv, z, gates, buf0, s0, conv_w, g_norm):
    b, l, n_qkv = qkv.shape
    nc = min(l // CHUNK, 4)
    rows = nc * CHUNK
    bb = _tile(b, max(GDN_CHUNKS_PER_STEP // nc, 1))
    n_z = z.shape[2]
    blk = lambda bi, ci: (bi, ci, 0)
    per_b3 = lambda bi, ci: (bi, 0, 0)
    per_b4 = lambda bi, ci: (bi, 0, 0, 0)
    return pl.pallas_call(
        _gdn_kernel,
        grid=(b // bb, l // rows),
        in_specs=[pl.BlockSpec((bb, rows, n_qkv), blk), pl.BlockSpec((bb, rows, n_z), blk),
                  pl.BlockSpec((bb, rows, LANES), blk),
                  pl.BlockSpec((bb, CONV_A - 1, n_qkv), per_b3),
                  pl.BlockSpec((bb, H_A, DK_A, DV_A), per_b4),
                  _const_spec(conv_w.shape), _const_spec(g_norm.shape)],
        out_specs=[pl.BlockSpec((bb, rows, n_z), blk),
                   pl.BlockSpec((bb, CONV_A - 1, n_qkv), per_b3),
                   pl.BlockSpec((bb, H_A, DK_A, DV_A), per_b4)],
        out_shape=[jax.ShapeDtypeStruct((b, l, n_z), F32),
                   jax.ShapeDtypeStruct((b, CONV_A - 1, n_qkv), F32),
                   jax.ShapeDtypeStruct((b, H_A, DK_A, DV_A), F32)],
        scratch_shapes=[pltpu.VMEM((bb, 8 + rows, n_qkv), F32), pltpu.VMEM((bb, H_A, DK_A, DV_A), F32)],
        compiler_params=_params("parallel", "arbitrary"),
        name="gdn",
    )(qkv, z, gates, buf0, s0, conv_w, g_norm)


FOX_GROUP = 256


def _fox_kernel(qa_ref, ka_ref, vt_ref, o_ref, s_a, s_b, m_s, l_s, acc_s):
    i = pl.program_id(2)
    tq = qa_ref.shape[3]
    ng = tq // FOX_GROUP
    m_s[...] = jnp.full(m_s.shape, -jnp.inf, F32)
    l_s[...] = jnp.zeros(l_s.shape, F32)
    acc_s[...] = jnp.zeros(acc_s.shape, F32)

    def produce(j, s_ref):
        kj = ka_ref[0, 0, j]
        for g in range(ng):
            s_ref[g] = _dot_nt(kj, qa_ref[0, 0, 0, g * FOX_GROUP:(g + 1) * FOX_GROUP, :])

    def consume(j, s_ref, diagonal):
        for g in range(ng):
            rows = (g + 1) * FOX_GROUP if diagonal else s_ref.shape[1]
            s = s_ref[g, 0:rows, :]
            if diagonal:
                kpos = lax.broadcasted_iota(jnp.int32, s.shape, 0)
                qpos = lax.broadcasted_iota(jnp.int32, s.shape, 1) + g * FOX_GROUP
                s = jnp.where(kpos <= qpos, s, -jnp.inf)
            m_old = m_s[g]
            m_new = jnp.maximum(m_old, jnp.max(s, axis=0, keepdims=True))
            p = jnp.exp2(s - m_new)
            corr = jnp.exp2(m_old - m_new)
            l_s[g] = l_s[g] * corr + jnp.sum(p, axis=0, keepdims=True)
            acc_s[g] = acc_s[g] * corr + _dot(vt_ref[0, 0, j, :, 0:rows], p.astype(BF16))
            m_s[g] = m_new

    produce(0, s_a)

    def pair(t, carry):
        j = 2 * t
        produce(j + 1, s_b)
        consume(j, s_a, False)
        produce(j + 2, s_a)
        consume(j + 1, s_b, False)
        return carry

    lax.fori_loop(0, i // 2, pair, 0)

    @pl.when(i % 2 == 0)
    def _():
        consume(i, s_a, True)

    @pl.when(i % 2 == 1)
    def _():
        produce(i, s_b)
        consume(i - 1, s_a, False)
        consume(i, s_b, True)

    for g in range(ng):
        o_ref[0, g * FOX_GROUP:(g + 1) * FOX_GROUP, :] = (acc_s[g] / l_s[g]).T


def _fox(qa, ka, vt):
    b, hb, nb, tq, _ = qa.shape
    ng = tq // FOX_GROUP
    return pl.pallas_call(
        _fox_kernel,
        grid=(b, hb, nb),
        in_specs=[pl.BlockSpec((1, 1, 1, tq, QK_AUG), lambda bi, h, i: (bi, h, i, 0, 0)),
                  pl.BlockSpec((1, 1, nb, tq, QK_AUG), lambda bi, h, i: (bi, h, 0, 0, 0)),
                  pl.BlockSpec((1, 1, nb, D_B, tq), lambda bi, h, i: (bi, h, 0, 0, 0))],
        out_specs=pl.BlockSpec((1, tq, D_B), lambda bi, h, i: (bi, i, h)),
        out_shape=jax.ShapeDtypeStruct((b, nb * tq, hb * D_B), F32),
        scratch_shapes=[pltpu.VMEM((ng, tq, FOX_GROUP), F32), pltpu.VMEM((ng, tq, FOX_GROUP), F32),
                        pltpu.VMEM((ng, 1, FOX_GROUP), F32), pltpu.VMEM((ng, 1, FOX_GROUP), F32),
                        pltpu.VMEM((ng, D_B, FOX_GROUP), F32)],
        compiler_params=_params("parallel", "parallel", "arbitrary"),
        name="fox",
    )(qa, ka, vt)


def _fox_dec_kernel(q_ref, kn_ref, vn_ref, kp_ref, vp_ref, lfp_ref, lfn_ref, gates_ref, o_ref):
    L = q_ref.shape[1]
    t_past = lfp_ref.shape[2]
    bf = lambda x: x.astype(BF16)
    heads = range(H_B)
    c_past = _cumsum_lanes(lfp_ref[0])
    total = c_past[:, t_past - 1:t_past]
    c_new = total + _cumsum_lanes(lfn_ref[0])
    c_col = _cumsum_rows(gates_ref[0])
    c_q = [total[h:h + 1, :] + c_col[:, 2 * H_A + h:2 * H_A + h + 1] for h in heads]

    scale = D_B ** -0.5
    r = lax.broadcasted_iota(jnp.int32, (L, L), 0)
    cc = lax.broadcasted_iota(jnp.int32, (L, L), 1)
    hs = [slice(h * D_B, (h + 1) * D_B) for h in heads]
    q = [bf(q_ref[0, :, hs[h]]) for h in heads]
    past = [pl.ds(h, t_past, stride=H_B) for h in heads]
    s_past = [_dot_nt(q[h], bf(kp_ref[0, past[h], :])) for h in heads]
    new = [pl.ds(h, L, stride=H_B) for h in heads]
    s_new = [_dot_nt(q[h], bf(kn_ref[0, new[h], :])) for h in heads]
    s_past = [s_past[h] * scale + c_q[h] - c_past[h:h + 1, :] for h in heads]
    s_new = [jnp.where(cc <= r, s_new[h] * scale + c_q[h] - c_new[h:h + 1, 0:L], -jnp.inf) for h in heads]
    m = [jnp.maximum(jnp.max(s_past[h], axis=1, keepdims=True), jnp.max(s_new[h], axis=1, keepdims=True))
         for h in heads]
    p_past = [jnp.exp(s_past[h] - m[h]) for h in heads]
    p_new = [jnp.exp(s_new[h] - m[h]) for h in heads]
    denom = [jnp.sum(p_past[h], axis=1, keepdims=True) + jnp.sum(p_new[h], axis=1, keepdims=True) for h in heads]
    o = [_dot(bf(p_past[h]), bf(vp_ref[0, past[h], :])) + _dot(bf(p_new[h]), bf(vn_ref[0, new[h], :])) for h in heads]
    for h in heads:
        o_ref[0, :, hs[h]] = o[h] / denom[h]


def _fox_dec(qn, kn, v, k_past, v_past, lfp, lfn, gates):
    b, l, n_b = qn.shape
    rows_past = k_past.shape[1]
    t_past = rows_past // H_B
    per3 = lambda bi: (bi, 0, 0)
    return pl.pallas_call(
        _fox_dec_kernel,
        grid=(b,),
        in_specs=[pl.BlockSpec((1, l, n_b), per3), pl.BlockSpec((1, l * H_B, D_B), per3),
                  pl.BlockSpec((1, l * H_B, D_B), per3),
                  pl.BlockSpec((1, rows_past, D_B), per3), pl.BlockSpec((1, rows_past, D_B), per3),
                  pl.BlockSpec((1, 8, t_past), per3), pl.BlockSpec((1, 8, LANES), per3),
                  pl.BlockSpec((1, l, LANES), per3)],
        out_specs=pl.BlockSpec((1, l, n_b), per3),
        out_shape=jax.ShapeDtypeStruct((b, l, n_b), F32),
        compiler_params=_params("parallel"),
        name="fox_dec",
    )(qn, kn, v, k_past, v_past, lfp, lfn, gates)


def _mlp(x, gain_ref, wu_ref, wd_ref, hid_ref):
    hn = _rms(x, gain_ref[...]).astype(BF16)
    d_ff = wu_ref.shape[1]
    ck = 1024
    for c in range(d_ff // ck):
        a = jnp.maximum(_dot(hn, wu_ref[:, c * ck:(c + 1) * ck]), 0.0)
        hid_ref[:, c * ck:(c + 1) * ck] = (a * a).astype(BF16)
    return x + _dot(hid_ref[...], wd_ref[...])


def _mix_mlp_kernel(x_ref, oa_ref, ob_ref, wo_ref, gain_ref, wu_ref, wd_ref, y_ref, hid_ref):
    n_a = oa_ref.shape[1]
    x = (x_ref[...] + _dot(oa_ref[...].astype(BF16), wo_ref[0:n_a, :])
         + _dot(ob_ref[...].astype(BF16), wo_ref[n_a:, :]))
    y_ref[...] = _mlp(x, gain_ref, wu_ref, wd_ref, hid_ref)


def _mix_mlp(x, oa, ob, wo, gain, wu, wd, tm):
    t, d = x.shape
    row = lambda i: (i, 0)
    return pl.pallas_call(
        _mix_mlp_kernel,
        grid=(t // tm,),
        in_specs=[pl.BlockSpec((tm, d), row), pl.BlockSpec((tm, oa.shape[1]), row),
                  pl.BlockSpec((tm, ob.shape[1]), row), _const_spec(wo.shape),
                  _const_spec(gain.shape), _const_spec(wu.shape), _const_spec(wd.shape)],
        out_specs=pl.BlockSpec((tm, d), row),
        out_shape=jax.ShapeDtypeStruct((t, d), F32),
        scratch_shapes=[pltpu.VMEM((tm, wu.shape[1]), BF16)],
        compiler_params=_params("parallel"),
        name="mix_mlp",
    )(x, oa, ob, wo, gain, wu, wd)


def _glu_kernel(x_ref, gain_ref, w_ref, g_ref):
    h = _rms(x_ref[...], gain_ref[...]).astype(BF16)
    d = g_ref.shape[1]
    never = pl.program_id(0) < 0
    n_strips = 4
    sw = d // n_strips
    gated = []
    for c in range(n_strips):
        hc = h if c < 2 else _after(never, h, gated[c - 2])
        a = _dot(hc, w_ref[:, c * sw:(c + 1) * sw])
        b = _dot(hc, w_ref[:, d + c * sw:d + (c + 1) * sw])
        gated.append(a * _sigmoid(b))
        g_ref[:, c * sw:(c + 1) * sw] = gated[c]


def _glu(x, gain, w, tm):
    t, d = x.shape
    row = lambda i: (i, 0)
    return pl.pallas_call(
        _glu_kernel,
        grid=(t // tm,),
        in_specs=[pl.BlockSpec((tm, d), row), _const_spec(gain.shape), _const_spec(w.shape)],
        out_specs=pl.BlockSpec((tm, d), row),
        out_shape=jax.ShapeDtypeStruct((t, d), F32),
        compiler_params=_params("parallel"),
        name="glu",
    )(x, gain, w)


HALO = 32


def _convmod_mlp_kernel(x_ref, g_ref, prev_ref, buf_ref, dw_ref, dwb_ref, lng_ref, lnb_ref, wpw_ref,
                        gain_ref, wu_ref, wd_ref, y_ref, xp_ref, cv_ref, act_ref, hid_ref, *, nt, ntiles):
    step = pl.program_id(0)
    i = jnp.minimum(step, ntiles - 1) % nt
    bb, tm, d = g_ref.shape
    rows = bb * tm

    @pl.when(step == 0)
    def _():
        act_ref[...] = jnp.zeros_like(act_ref)

    x = x_ref[...].reshape(rows, d) + _dot(act_ref[...], wpw_ref[...])
    hn = _rms(x, gain_ref[...]).astype(BF16)

    xp_ref[:, 0:HALO, :] = jnp.where(i == 0, buf_ref[...], prev_ref[...])
    xp_ref[:, HALO:HALO + tm, :] = g_ref[...]
    n = HALO + tm
    first = HALO - (CONV_C - 1)
    n_chunks = wu_ref.shape[1] // 1024
    strip = d // (2 * n_chunks)
    after = functools.partial(_after, step < 0)

    def conv_strip(c):
        cs = c * strip
        done = []
        for bi in range(bb):
            xp = xp_ref[bi, :, cs:cs + strip]
            acc = None
            for r in range(8):
                xr = xp if r == 0 else pltpu.roll(xp, n - r, axis=0)
                for a in range((first + CONV_C - 1) // 8 + 1):
                    j = 8 * a + r - first
                    if 0 <= j < CONV_C:
                        term = xr[8 * a:8 * a + tm, :] * dw_ref[j:j + 1, cs:cs + strip]
                        acc = term if acc is None else acc + term
            cv_ref[bi * tm:(bi + 1) * tm, cs:cs + strip] = acc
            done.append(acc)
        return jnp.concatenate(done, axis=0)

    for c in range(n_chunks):
        cols = slice(c * 1024, (c + 1) * 1024)
        up = jnp.maximum(_dot(after(hn, conv_strip(c)), wu_ref[:, cols]), 0.0)
        hid_ref[:, cols] = (up * up).astype(BF16)
    y = x
    for c in range(n_chunks):
        cols = slice(c * 1024, (c + 1) * 1024)
        y = y + _dot(after(hid_ref[:, cols], conv_strip(n_chunks + c)), wd_ref[cols, :])

    cv = cv_ref[...] + dwb_ref[...]
    mu = jnp.mean(cv, axis=-1, keepdims=True)
    xc = cv - mu
    ln = xc * lax.rsqrt(jnp.mean(xc * xc, axis=-1, keepdims=True) + EPS) * lng_ref[...] + lnb_ref[...]
    act = (ln * _sigmoid(ln)).astype(BF16)
    act_ref[...] = act
    y_ref[...] = after(y, act).reshape(bb, tm, d)


def _convmod_mlp(x, g, buf, dw, dwb, lng, lnb, wpw, gain, wu, wd, tm, bb):
    b, l, d = x.shape
    nt = l // tm
    ntiles = (b // bb) * nt
    per = tm // HALO
    conv_tile = lambda s: jnp.minimum(s, ntiles - 1)
    mlp_tile = lambda s: jnp.maximum(s - 1, 0)
    cur = lambda s: (conv_tile(s) // nt, conv_tile(s) % nt, 0)
    prev = lambda s: (conv_tile(s) // nt, jnp.maximum((conv_tile(s) % nt) * per - 1, 0), 0)
    first = lambda s: (conv_tile(s) // nt, 0, 0)
    out = lambda s: (mlp_tile(s) // nt, mlp_tile(s) % nt, 0)
    return pl.pallas_call(
        functools.partial(_convmod_mlp_kernel, nt=nt, ntiles=ntiles),
        grid=(ntiles + 1,),
        in_specs=[pl.BlockSpec((bb, tm, d), out), pl.BlockSpec((bb, tm, d), cur),
                  pl.BlockSpec((bb, HALO, d), prev), pl.BlockSpec((bb, HALO, d), first),
                  _const_spec(dw.shape), _const_spec(dwb.shape), _const_spec(lng.shape),
                  _const_spec(lnb.shape), _const_spec(wpw.shape), _const_spec(gain.shape),
                  _const_spec(wu.shape), _const_spec(wd.shape)],
        out_specs=pl.BlockSpec((bb, tm, d), out),
        out_shape=jax.ShapeDtypeStruct((b, l, d), F32),
        scratch_shapes=[pltpu.VMEM((bb, HALO + tm, d), F32), pltpu.VMEM((bb * tm, d), F32),
                        pltpu.VMEM((bb * tm, d), BF16), pltpu.VMEM((bb * tm, wu.shape[1]), BF16)],
        compiler_params=_params("arbitrary"),
        name="convmod_mlp",
    )(x, g, g, buf, dw, dwb, lng, lnb, wpw, gain, wu, wd)


def _tile(n, pref):
    t = min(n, pref)
    assert n % t == 0
    return t


def _pad_lanes(v, offset=0):
    return jnp.zeros((1, LANES), F32).at[0, offset:offset + v.shape[0]].set(v.astype(F32))


def _trunk(x, past, w):
    b, l, d = x.shape
    t = b * l
    x2 = x.reshape(t, d)
    tm = _tile(t, 512)

    tq = _tile(l, 1024) if past is None else None
    qkv, z, kn, v, gates, *q_side = _inproj(x, w["norm_mix_e"], w["w_main"], w["a_log"], w["dt_bias"],
                                            w["b_f"], w["q_norm_b"], w["k_norm_b"], tm, tq)
    r3 = lambda a: a.reshape((b, a.shape[0] // b) + a.shape[1:])
    if past is None:
        buf_a = jnp.zeros((b, CONV_A - 1, qkv.shape[-1]), F32)
        s0 = jnp.zeros((b, H_A, DK_A, DV_A), F32)
    else:
        buf_a, s0 = past["conv_a"], past["delta"]
    o_a, nbuf_a, s_a = _gdn(r3(qkv), r3(z), r3(gates), buf_a, s0, w["conv_a_w"], w["g_norm_a"])

    if past is None:
        qa, ka, vt = q_side
        blocks = lambda a: a.reshape(b, H_B, l // tq, tq, QK_AUG)
        o_b = _fox(blocks(qa), blocks(ka), vt)
    else:
        qn, = q_side
        lfp = jnp.pad(jnp.transpose(past["lf_b"], (0, 2, 1)), ((0, 0), (0, 8 - H_B), (0, 0)))
        lf_new = r3(gates)[:, :, 2 * H_A:2 * H_A + H_B]
        lfn = jnp.pad(jnp.transpose(lf_new, (0, 2, 1)), ((0, 0), (0, 8 - H_B), (0, LANES - l)))
        flat = lambda c: c.reshape(b, c.shape[1] * H_B, D_B)
        o_b = _fox_dec(r3(qn), r3(kn), r3(v), flat(past["k_b"]), flat(past["v_b"]), lfp, lfn, r3(gates))

    x2 = _mix_mlp(x2, o_a.reshape(t, -1), o_b.reshape(t, -1), w["w_out"], w["norm_mlp0"],
                  w["w_up0"], w["w_down0"], tm)

    g = _glu(x2, w["norm_mix_o"], w["w_glu"], tm)
    g3 = g.reshape(b, l, d)
    if past is None:
        buf_c = jnp.zeros((b, CONV_C - 1, d), F32)
    else:
        buf_c = past["conv_c"]
    buf_pad = jnp.concatenate([jnp.zeros((b, HALO - (CONV_C - 1), d), F32), buf_c], axis=1)
    tc = _tile(l, 512)
    y = _convmod_mlp(x2.reshape(b, l, d), g3, buf_pad, w["dw_c"], w["dw_c_b"], w["ln_c_g"], w["ln_c_b"],
                     w["w_pw_c"], w["norm_mlp1"], w["w_up1"], w["w_down1"], tc, _tile(b, max(512 // tc, 1)))
    nbuf_c = jnp.concatenate([buf_c, g3], axis=1)[:, l:, :] if l < CONV_C - 1 else g3[:, l - (CONV_C - 1):, :]

    new = {
        "conv_a": nbuf_a[None], "delta": s_a[None],
        "k_b": kn.reshape(1, b, l, H_B, D_B), "v_b": v.reshape(1, b, l, H_B, D_B),
        "lf_b": r3(gates)[None, :, :, 2 * H_A:2 * H_A + H_B], "conv_c": nbuf_c[None],
    }
    return y, new


def kernel(x_prompt, x_sample, state_conv_a, state_delta_a, cache_k_b, cache_v_b, cache_logf_b, state_conv_c,
           norm_mix_e, w_in_e, b_f, conv_a_w, a_log, dt_bias, g_norm_a, q_norm_b, k_norm_b, w_out_e,
           norm_mix_o, w_glu, dw_c, dw_c_b, ln_c_g, ln_c_b, w_pw_c, norm_mlp, w_up, w_down):
    n_qkv = H_A * (2 * DK_A + DV_A)
    n_z = H_A * DV_A
    n_b = H_B * D_B
    w_in = w_in_e[0]
    o_a = n_qkv + n_z
    o_q = o_a + 2 * H_A
    o_f = o_q + 3 * n_b
    w_main = jnp.concatenate([w_in[:, :o_a], w_in[:, o_q:o_f], w_in[:, o_a:o_q], w_in[:, o_f:o_f + H_B],
                              jnp.zeros((w_in.shape[0], LANES - 2 * H_A - H_B), F32)], axis=1).astype(BF16)
    row = lambda a: a.reshape(1, -1).astype(F32)
    w = {
        "norm_mix_e": row(norm_mix_e[0]), "w_main": w_main,
        "a_log": _pad_lanes(a_log[0]), "dt_bias": _pad_lanes(dt_bias[0]),
        "b_f": _pad_lanes(b_f[0], 2 * H_A),
        "q_norm_b": row(q_norm_b[0]), "k_norm_b": row(k_norm_b[0]),
        "conv_a_w": conv_a_w[0], "g_norm_a": row(g_norm_a[0]),
        "w_out": w_out_e[0].astype(BF16),
        "norm_mlp0": row(norm_mlp[0]), "w_up0": w_up[0].astype(BF16), "w_down0": w_down[0].astype(BF16),
        "norm_mix_o": row(norm_mix_o[0]), "w_glu": w_glu[0].astype(BF16),
        "dw_c": dw_c[0], "dw_c_b": row(dw_c_b[0]), "ln_c_g": row(ln_c_g[0]), "ln_c_b": row(ln_c_b[0]),
        "w_pw_c": w_pw_c[0].astype(BF16),
        "norm_mlp1": row(norm_mlp[1]), "w_up1": w_up[1].astype(BF16), "w_down1": w_down[1].astype(BF16),
    }
    y_prompt, ps = _trunk(x_prompt, None, w)
    past = {"conv_a": state_conv_a[0], "delta": state_delta_a[0], "k_b": cache_k_b[0], "v_b": cache_v_b[0],
            "lf_b": cache_logf_b[0], "conv_c": state_conv_c[0]}
    y_sample, ss = _trunk(x_sample, past, w)
    return (y_prompt, y_sample,
            ps["conv_a"], ps["delta"], ps["k_b"], ps["v_b"], ps["lf_b"], ps["conv_c"],
            ss["conv_a"], ss["delta"], ss["k_b"], ss["v_b"], ss["lf_b"], ss["conv_c"])
```

```python
import functools
import math

import jax
import jax.numpy as jnp
from jax import lax
from jax.experimental import pallas as pl
from jax.experimental.pallas import tpu as pltpu

F32 = jnp.float32
BF16 = jnp.bfloat16
EPS = 1e-6
LANES = 128
VMEM_LIMIT = 56 * 1024 * 1024

H_A = 4
DK_A = 128
DV_A = 128
CONV_A = 4
H_B = 4
D_B = 128
CONV_C = 31
CHUNK = 64
QK_AUG = 2 * D_B
LOG2E = 1.4426950408889634


def _params(*sem, flags=None):
    return pltpu.CompilerParams(dimension_semantics=sem, vmem_limit_bytes=VMEM_LIMIT, flags=flags)


def _const_spec(shape):
    nd = len(shape)
    return pl.BlockSpec(shape, lambda *_: (0,) * nd, pipeline_mode=pl.Buffered(1))


def _rms(x, gain):
    return x * lax.rsqrt(jnp.mean(x * x, axis=-1, keepdims=True) + EPS) * gain


def _sigmoid(x):
    return 1.0 / (1.0 + jnp.exp(-x))


def _softplus(x):
    return jnp.maximum(x, 0.0) + jnp.log(1.0 + jnp.exp(-jnp.abs(x)))


def _dot(a, b):
    return jnp.dot(a, b, preferred_element_type=F32)


def _dot_nt(a, b):
    return lax.dot_general(a, b, (((1,), (1,)), ((), ())), preferred_element_type=F32)


def _dot_tn(a, b):
    return lax.dot_general(a, b, (((0,), (0,)), ((), ())), preferred_element_type=F32)


def _split3(x):
    hi = x.astype(BF16)
    r = x - hi.astype(F32)
    mid = r.astype(BF16)
    lo = (r - mid.astype(F32)).astype(BF16)
    return hi, mid, lo


def _after(never, value, witness):
    w = witness.shape[1]
    head = jnp.where(never, witness.astype(value.dtype), value[:, 0:w])
    return head if w == value.shape[1] else jnp.concatenate([head, value[:, w:]], axis=1)


def _cumsum_rows(x):
    n = x.shape[0]
    row = lax.broadcasted_iota(jnp.int32, x.shape, 0)
    s = 1
    while s < n:
        x = x + jnp.where(row >= s, pltpu.roll(x, s, axis=0), 0.0)
        s *= 2
    return x


def _cumsum_lanes(x):
    n = x.shape[1]
    col = lax.broadcasted_iota(jnp.int32, x.shape, 1)
    s = 1
    while s < n:
        x = x + jnp.where(col >= s, pltpu.roll(x, s, axis=1), 0.0)
        s *= 2
    return x


def _inproj_kernel(x_ref, gain_ref, wm_ref, alog_ref, dtb_ref, bf_ref, qg_ref, kg_ref, *refs, nt):
    if nt is None:
        qkv_ref, z_ref, kn_ref, v_ref, gates_ref, qn_ref = refs
    else:
        qkv_ref, z_ref, kn_ref, v_ref, gates_ref, qa_ref, ka_ref, vt_ref, carry = refs
    tm = x_ref.shape[0]
    h = _rms(x_ref[...], gain_ref[...]).astype(BF16)
    n_qkv = qkv_ref.shape[1]
    n_z = z_ref.shape[1]
    n_b = H_B * D_B
    off = n_qkv + n_z
    vg = _dot(h, wm_ref[:, off + 2 * n_b:off + 3 * n_b + LANES])
    q = _dot(h, wm_ref[:, off:off + n_b])
    k = _dot(h, wm_ref[:, off + n_b:off + 2 * n_b])
    g = vg[:, n_b:n_b + LANES]
    lane = lax.broadcasted_iota(jnp.int32, g.shape, 1)
    lg = -jnp.exp(alog_ref[...]) * _softplus(g + dtb_ref[...])
    beta = _sigmoid(g)
    lf = -_softplus(-(g + bf_ref[...]))
    gates = jnp.where(lane < H_A, lg, jnp.where(lane < 2 * H_A, beta,
                                                jnp.where(lane < 2 * H_A + H_B, lf, 0.0)))
    gates_ref[...] = gates
    if nt is not None:
        @pl.when(pl.program_id(0) % nt == 0)
        def _():
            carry[...] = jnp.zeros_like(carry)

        csum = _cumsum_rows(gates) + carry[...]
        carry[...] = csum[tm - 1:tm, :]
    never = pl.program_id(0) < 0
    piece = n_qkv // (H_B - 1)
    qkv_ref[:, 0:piece] = _dot(h, wm_ref[:, 0:piece])
    h_tied = h
    for hh in range(H_B):
        sl = slice(hh * D_B, (hh + 1) * D_B)
        qn = _rms(q[:, sl], qg_ref[...])
        kn = _rms(k[:, sl], kg_ref[...])
        heads_rows = pl.ds(hh, tm, stride=H_B)
        kn_ref[heads_rows, :] = kn
        v_ref[heads_rows, :] = vg[:, sl]
        if nt is None:
            qn_ref[:, sl] = qn
        else:
            c2 = jnp.broadcast_to(csum[:, 2 * H_A + hh:2 * H_A + hh + 1] * LOG2E, (tm, LANES))
            hi, mid, lo = (piece.astype(F32) for piece in _split3(c2))
            q_aug = jnp.where(lane == 0, hi, jnp.where(lane == 1, mid, jnp.where(lane == 2, lo,
                              jnp.where(lane < 6, 1.0, 0.0)))).astype(BF16)
            k_aug = jnp.where(lane < 3, 1.0, jnp.where(lane == 3, -hi, jnp.where(lane == 4, -mid,
                              jnp.where(lane == 5, -lo, 0.0)))).astype(BF16)
            q_main = (qn * (D_B ** -0.5 * LOG2E)).astype(BF16)
            k_main = kn.astype(BF16)
            qa_ref[0, hh, :, 0:D_B] = q_main
            qa_ref[0, hh, :, D_B:QK_AUG] = q_aug
            ka_ref[0, hh, :, 0:D_B] = k_main
            ka_ref[0, hh, :, D_B:QK_AUG] = k_aug
            vt_ref[0, hh, 0] = vg[:, sl].T.astype(BF16)
            for witness in (q_main, q_aug, k_main, k_aug):
                h_tied = _after(never, h_tied, witness)
        if hh < H_B - 2:
            cols = slice((hh + 1) * piece, (hh + 2) * piece)
            qkv_ref[:, cols] = _dot(h_tied, wm_ref[:, cols])
        elif hh == H_B - 2:
            z_ref[...] = _dot(h_tied, wm_ref[:, n_qkv:n_qkv + n_z])


def _inproj(x, gain, wm, alog, dtb, bfr, qg, kg, tm, fox_block=None):
    b, l, d = x.shape
    t = b * l
    n_qkv = H_A * (2 * DK_A + DV_A)
    n_z = H_A * DV_A
    n_b = H_B * D_B
    row = lambda i: (i, 0)
    outs = [(1, n_qkv), (1, n_z), (H_B, D_B), (H_B, D_B), (1, LANES)]
    out_specs = [pl.BlockSpec((tm * r, n), row) for r, n in outs]
    out_shape = [jax.ShapeDtypeStruct((t * r, n), F32) for r, n in outs]
    scratch = []
    if fox_block is None:
        nt = None
        out_specs.append(pl.BlockSpec((tm, n_b), row))
        out_shape.append(jax.ShapeDtypeStruct((t, n_b), F32))
    else:
        nt = l // tm
        per = fox_block // tm
        aug = pl.BlockSpec((1, H_B, tm, QK_AUG), lambda i: (i // nt, 0, i % nt, 0))
        out_specs += [aug, aug, pl.BlockSpec((1, H_B, 1, D_B, tm),
                                             lambda i: (i // nt, 0, (i % nt) // per, 0, (i % nt) % per))]
        out_shape += [jax.ShapeDtypeStruct((b, H_B, l, QK_AUG), BF16),
                      jax.ShapeDtypeStruct((b, H_B, l, QK_AUG), BF16),
                      jax.ShapeDtypeStruct((b, H_B, l // fox_block, D_B, fox_block), BF16)]
        scratch = [pltpu.VMEM((1, LANES), F32)]
    return pl.pallas_call(
        functools.partial(_inproj_kernel, nt=nt),
        grid=(t // tm,),
        in_specs=[pl.BlockSpec((tm, d), row), _const_spec(gain.shape), _const_spec(wm.shape),
                  _const_spec(alog.shape), _const_spec(dtb.shape),
                  _const_spec(bfr.shape), _const_spec(qg.shape), _const_spec(kg.shape)],
        out_specs=out_specs,
        out_shape=out_shape,
        scratch_shapes=scratch,
        compiler_params=_params("arbitrary"),
        name="inproj",
    )(x.reshape(t, d), gain, wm, alog, dtb, bfr, qg, kg)


def _each(f, *lists):
    return [f(*xs) for xs in zip(*lists)]


def _mm(x, y):
    return _dot(x.astype(BF16), y.astype(BF16))


def _unit_lower_inverse_minus_eye(a_list, same_block):
    d = _each(lambda a: jnp.where(same_block, a, 0.0), a_list)
    nn = _each(lambda a, dd: a - dd, a_list, d)
    d2 = _each(lambda x: _mm(x, x), d)
    d4 = _each(lambda x: _mm(x, x), d2)
    e = _each(lambda dd, x2: x2 - dd - _mm(dd, x2), d, d2)
    d8 = _each(lambda x: _mm(x, x), d4)
    e = _each(lambda ee, x4: ee + x4 + _mm(ee, x4), e, d4)
    e = _each(lambda ee, x8: ee + x8 + _mm(ee, x8), e, d8)
    m = _each(lambda ee, n: n + _mm(ee, n), e, nn)
    m2 = _each(lambda x: _mm(x, x), m)
    ex = _each(lambda ee, mm: ee - mm - _mm(mm, ee), e, m)
    return _each(lambda x, mm2: x + mm2 + _mm(mm2, x), ex, m2)


def _gdn_kernel(qkv_ref, z_ref, gates_ref, buf0_ref, s0_ref, cw_ref, gn_ref,
                o_ref, nbuf_ref, sout_ref, xbuf, state):
    c = pl.program_id(1)
    L = CHUNK
    bb, rows = qkv_ref.shape[0], qkv_ref.shape[1]
    nc = rows // L
    pad = 8
    bf = lambda x: x.astype(BF16)

    @pl.when(c == 0)
    def _():
        xbuf[:, pad - (CONV_A - 1):pad, :] = buf0_ref[...]
        state[...] = s0_ref[...]

    xbuf[:, pad:pad + rows, :] = qkv_ref[...]
    w = cw_ref[...]
    ys = []
    for bi in range(bb):
        y = xbuf[bi, pad:pad + rows, :] * w[CONV_A - 1:CONV_A, :]
        for i in range(1, CONV_A):
            y = y + xbuf[bi, pad - i:pad - i + rows, :] * w[CONV_A - 1 - i:CONV_A - i, :]
        ys.append(y * _sigmoid(y))
    tail = xbuf[:, pad + rows - (CONV_A - 1):pad + rows, :]
    xbuf[:, pad - (CONV_A - 1):pad, :] = tail
    nbuf_ref[...] = tail

    r = lax.broadcasted_iota(jnp.int32, (L, L), 0)
    cc = lax.broadcasted_iota(jnp.int32, (L, L), 1)
    causal = r >= cc
    strict = r > cc
    same_block = (r // 16) == (cc // 16)

    units = [(bi, ci, h) for bi in range(bb) for ci in range(nc) for h in range(H_A)]
    gam = {}
    for bi in range(bb):
        for ci in range(nc):
            gates = gates_ref[bi, ci * L:(ci + 1) * L, :]
            gam_all = _cumsum_rows(gates)
            gam_t = jnp.concatenate([gam_all, jnp.zeros_like(gam_all)], axis=0).T
            gam[bi, ci] = (gates, gam_all, gam_t)

    def normalized(bi, ci, h):
        rs = slice(ci * L, (ci + 1) * L)
        q = ys[bi][rs, h * DK_A:(h + 1) * DK_A]
        k = ys[bi][rs, H_A * DK_A + h * DK_A:H_A * DK_A + (h + 1) * DK_A]
        v = ys[bi][rs, 2 * H_A * DK_A + h * DV_A:2 * H_A * DK_A + (h + 1) * DV_A]
        q = q * lax.rsqrt(jnp.sum(q * q, axis=-1, keepdims=True) + EPS) * (DK_A ** -0.5)
        k = k * lax.rsqrt(jnp.sum(k * k, axis=-1, keepdims=True) + EPS)
        return q, k, v

    qkv_n = [normalized(*u) for u in units]
    q = [t[0] for t in qkv_n]
    k = [t[1] for t in qkv_n]
    v = [t[2] for t in qkv_n]
    gcol = [gam[bi, ci][1][:, h:h + 1] for bi, ci, h in units]
    grow = [gam[bi, ci][2][h:h + 1, 0:L] for bi, ci, h in units]
    beta = [gam[bi, ci][0][:, H_A + h:H_A + h + 1] for bi, ci, h in units]
    g_last = [gam[bi, ci][1][L - 1:L, h:h + 1] for bi, ci, h in units]
    decay = _each(lambda gc, gr: jnp.exp(jnp.where(causal, gc - gr, -jnp.inf)), gcol, grow)
    eg = _each(jnp.exp, gcol)
    kb = _each(bf, k)
    qb = _each(bf, q)
    kk = _each(_dot_nt, kb, kb)
    qk = _each(_dot_nt, qb, kb)
    a_mat = _each(lambda b_, d_, kk_: jnp.where(strict, b_ * d_ * kk_, 0.0), beta, decay, kk)
    p_mat = _each(lambda d_, qk_: bf(d_ * qk_), decay, qk)
    e_inv = _unit_lower_inverse_minus_eye(a_mat, same_block)
    x = _each(lambda b_, v_, eg_, k_: jnp.concatenate([b_ * v_, (b_ * eg_) * k_], axis=1), beta, v, eg, k)
    tx = _each(lambda x_, e_: x_ + _mm(e_, x_), x, e_inv)
    u0 = [t[:, 0:DV_A] for t in tx]
    wq = _each(lambda t, eg_, q_: jnp.concatenate([bf(t[:, DV_A:]), bf(eg_ * q_)], axis=0), tx, eg, q)
    kd = _each(lambda gl, gc, k_: bf(jnp.exp(gl - gc) * k_), g_last, gcol, k)
    dec = _each(jnp.exp, g_last)

    chains = [(bi, h) for bi in range(bb) for h in range(H_A)]
    s = [state[bi, h] for bi, h in chains]
    for ci in range(nc):
        ids = [units.index((bi, ci, h)) for bi, h in chains]
        ws_qs = [_dot(wq[u], bf(s_)) for u, s_ in zip(ids, s)]
        u_b = [bf(u0[u] - t[0:L]) for u, t in zip(ids, ws_qs)]
        pu = [_dot(p_mat[u], ub_) for u, ub_ in zip(ids, u_b)]
        ku = [_dot_tn(kd[u], ub_) for u, ub_ in zip(ids, u_b)]
        s = [dec[u] * s_ + ku_ for u, s_, ku_ in zip(ids, s, ku)]
        for (bi, h), t, pu_ in zip(chains, ws_qs, pu):
            zz = z_ref[bi, ci * L:(ci + 1) * L, h * DV_A:(h + 1) * DV_A]
            o_ref[bi, ci * L:(ci + 1) * L, h * DV_A:(h + 1) * DV_A] = (
                _rms(t[L:2 * L] + pu_, gn_ref[...]) * (zz * _sigmoid(zz)))
    for (bi, h), s_ in zip(chains, s):
        state[bi, h] = s_
        sout_ref[bi, h] = s_


GDN_CHUNKS_PER_STEP = 8


def _gdn(qkv, z, gates, buf0, s0, conv_w, g_norm):
    b, l, n_qkv = qkv.shape
    nc = min(l // CHUNK, 4)
    rows = nc * CHUNK
    bb = _tile(b, max(GDN_CHUNKS_PER_STEP // nc, 1))
    n_z = z.shape[2]
    blk = lambda bi, ci: (bi, ci, 0)
    per_b3 = lambda bi, ci: (bi, 0, 0)
    per_b4 = lambda bi, ci: (bi, 0, 0, 0)
    return pl.pallas_call(
        _gdn_kernel,
        grid=(b // bb, l // rows),
        in_specs=[pl.BlockSpec((bb, rows, n_qkv), blk), pl.BlockSpec((bb, rows, n_z), blk),
                  pl.BlockSpec((bb, rows, LANES), blk),
                  pl.BlockSpec((bb, CONV_A - 1, n_qkv), per_b3),
                  pl.BlockSpec((bb, H_A, DK_A, DV_A), per_b4),
                  _const_spec(conv_w.shape), _const_spec(g_norm.shape)],
        out_specs=[pl.BlockSpec((bb, rows, n_z), blk),
                   pl.BlockSpec((bb, CONV_A - 1, n_qkv), per_b3),
                   pl.BlockSpec((bb, H_A, DK_A, DV_A), per_b4)],
        out_shape=[jax.ShapeDtypeStruct((b, l, n_z), F32),
                   jax.ShapeDtypeStruct((b, CONV_A - 1, n_qkv), F32),
                   jax.ShapeDtypeStruct((b, H_A, DK_A, DV_A), F32)],
        scratch_shapes=[pltpu.VMEM((bb, 8 + rows, n_qkv), F32), pltpu.VMEM((bb, H_A, DK_A, DV_A), F32)],
        compiler_params=_params("parallel", "arbitrary"),
        name="gdn",
    )(qkv, z, gates, buf0, s0, conv_w, g_norm)


FOX_GROUP = 256


def _fox_kernel(qa_ref, ka_ref, vt_ref, o_ref, s_a, s_b, mx_a, mx_b, m_s, l_s, acc_s):
    i = pl.program_id(2)
    tq = qa_ref.shape[3]
    ng = tq // FOX_GROUP
    m_s[...] = jnp.full(m_s.shape, -jnp.inf, F32)
    l_s[...] = jnp.zeros(l_s.shape, F32)
    acc_s[...] = jnp.zeros(acc_s.shape, F32)

    def produce(j, s_ref, mx_ref):
        kj = ka_ref[0, 0, j]
        for g in range(ng):
            s = _dot_nt(kj, qa_ref[0, 0, 0, g * FOX_GROUP:(g + 1) * FOX_GROUP, :])
            s_ref[g] = s
            mx_ref[g] = jnp.max(s, axis=0, keepdims=True)

    def consume(j, s_ref, mx_ref, diagonal):
        for g in range(ng):
            rows = (g + 1) * FOX_GROUP if diagonal else s_ref.shape[1]
            s = s_ref[g, 0:rows, :]
            if diagonal:
                kpos = lax.broadcasted_iota(jnp.int32, s.shape, 0)
                qpos = lax.broadcasted_iota(jnp.int32, s.shape, 1) + g * FOX_GROUP
                s = jnp.where(kpos <= qpos, s, -jnp.inf)
            m_old = m_s[g]
            m_new = jnp.maximum(m_old, jnp.max(s, axis=0, keepdims=True) if diagonal else mx_ref[g])
            p = jnp.exp2(s - m_new)
            corr = jnp.exp2(m_old - m_new)
            l_s[g] = l_s[g] * corr + jnp.sum(p, axis=0, keepdims=True)
            acc_s[g] = acc_s[g] * corr + _dot(vt_ref[0, 0, j, :, 0:rows], p.astype(BF16))
            m_s[g] = m_new

    produce(0, s_a, mx_a)

    def pair(t, carry):
        j = 2 * t
        produce(j + 1, s_b, mx_b)
        consume(j, s_a, mx_a, False)
        produce(j + 2, s_a, mx_a)
        consume(j + 1, s_b, mx_b, False)
        return carry

    lax.fori_loop(0, i // 2, pair, 0)

    @pl.when(i % 2 == 0)
    def _():
        consume(i, s_a, mx_a, True)

    @pl.when(i % 2 == 1)
    def _():
        produce(i, s_b, mx_b)
        consume(i - 1, s_a, mx_a, False)
        consume(i, s_b, mx_b, True)

    for g in range(ng):
        o_ref[0, g * FOX_GROUP:(g + 1) * FOX_GROUP, :] = (acc_s[g] / l_s[g]).T


def _fox(qa, ka, vt):
    b, hb, nb, tq, _ = qa.shape
    ng = tq // FOX_GROUP
    return pl.pallas_call(
        _fox_kernel,
        grid=(b, hb, nb),
        in_specs=[pl.BlockSpec((1, 1, 1, tq, QK_AUG), lambda bi, h, i: (bi, h, i, 0, 0)),
                  pl.BlockSpec((1, 1, nb, tq, QK_AUG), lambda bi, h, i: (bi, h, 0, 0, 0)),
                  pl.BlockSpec((1, 1, nb, D_B, tq), lambda bi, h, i: (bi, h, 0, 0, 0))],
        out_specs=pl.BlockSpec((1, tq, D_B), lambda bi, h, i: (bi, i, h)),
        out_shape=jax.ShapeDtypeStruct((b, nb * tq, hb * D_B), F32),
        scratch_shapes=[pltpu.VMEM((ng, tq, FOX_GROUP), F32), pltpu.VMEM((ng, tq, FOX_GROUP), F32),
                        pltpu.VMEM((ng, 1, FOX_GROUP), F32), pltpu.VMEM((ng, 1, FOX_GROUP), F32),
                        pltpu.VMEM((ng, 1, FOX_GROUP), F32), pltpu.VMEM((ng, 1, FOX_GROUP), F32),
                        pltpu.VMEM((ng, D_B, FOX_GROUP), F32)],
        compiler_params=_params("parallel", "parallel", "arbitrary"),
        name="fox",
    )(qa, ka, vt)


def _fox_dec_kernel(q_ref, kn_ref, vn_ref, kp_ref, vp_ref, lfp_ref, lfn_ref, gates_ref, o_ref):
    L = q_ref.shape[1]
    t_past = lfp_ref.shape[2]
    bf = lambda x: x.astype(BF16)
    heads = range(H_B)
    c_past = _cumsum_lanes(lfp_ref[0])
    total = c_past[:, t_past - 1:t_past]
    c_new = total + _cumsum_lanes(lfn_ref[0])
    c_col = _cumsum_rows(gates_ref[0])
    c_q = [total[h:h + 1, :] + c_col[:, 2 * H_A + h:2 * H_A + h + 1] for h in heads]

    scale = D_B ** -0.5
    r = lax.broadcasted_iota(jnp.int32, (L, L), 0)
    cc = lax.broadcasted_iota(jnp.int32, (L, L), 1)
    hs = [slice(h * D_B, (h + 1) * D_B) for h in heads]
    q = [bf(q_ref[0, :, hs[h]]) for h in heads]
    past = [pl.ds(h, t_past, stride=H_B) for h in heads]
    s_past = [_dot_nt(q[h], bf(kp_ref[0, past[h], :])) for h in heads]
    new = [pl.ds(h, L, stride=H_B) for h in heads]
    s_new = [_dot_nt(q[h], bf(kn_ref[0, new[h], :])) for h in heads]
    s_past = [s_past[h] * scale + c_q[h] - c_past[h:h + 1, :] for h in heads]
    s_new = [jnp.where(cc <= r, s_new[h] * scale + c_q[h] - c_new[h:h + 1, 0:L], -jnp.inf) for h in heads]
    m = [jnp.maximum(jnp.max(s_past[h], axis=1, keepdims=True), jnp.max(s_new[h], axis=1, keepdims=True))
         for h in heads]
    p_past = [jnp.exp(s_past[h] - m[h]) for h in heads]
    p_new = [jnp.exp(s_new[h] - m[h]) for h in heads]
    denom = [jnp.sum(p_past[h], axis=1, keepdims=True) + jnp.sum(p_new[h], axis=1, keepdims=True) for h in heads]
    o = [_dot(bf(p_past[h]), bf(vp_ref[0, past[h], :])) + _dot(bf(p_new[h]), bf(vn_ref[0, new[h], :])) for h in heads]
    for h in heads:
        o_ref[0, :, hs[h]] = o[h] / denom[h]


def _fox_dec(qn, kn, v, k_past, v_past, lfp, lfn, gates):
    b, l, n_b = qn.shape
    rows_past = k_past.shape[1]
    t_past = rows_past // H_B
    per3 = lambda bi: (bi, 0, 0)
    return pl.pallas_call(
        _fox_dec_kernel,
        grid=(b,),
        in_specs=[pl.BlockSpec((1, l, n_b), per3), pl.BlockSpec((1, l * H_B, D_B), per3),
                  pl.BlockSpec((1, l * H_B, D_B), per3),
                  pl.BlockSpec((1, rows_past, D_B), per3), pl.BlockSpec((1, rows_past, D_B), per3),
                  pl.BlockSpec((1, 8, t_past), per3), pl.BlockSpec((1, 8, LANES), per3),
                  pl.BlockSpec((1, l, LANES), per3)],
        out_specs=pl.BlockSpec((1, l, n_b), per3),
        out_shape=jax.ShapeDtypeStruct((b, l, n_b), F32),
        compiler_params=_params("parallel"),
        name="fox_dec",
    )(qn, kn, v, k_past, v_past, lfp, lfn, gates)


def _mlp(x, gain_ref, wu_ref, wd_ref, hid_ref):
    hn = _rms(x, gain_ref[...]).astype(BF16)
    d_ff = wu_ref.shape[1]
    ck = 1024
    for c in range(d_ff // ck):
        a = jnp.maximum(_dot(hn, wu_ref[:, c * ck:(c + 1) * ck]), 0.0)
        hid_ref[:, c * ck:(c + 1) * ck] = (a * a).astype(BF16)
    return x + _dot(hid_ref[...], wd_ref[...])


def _mix_mlp_kernel(x_ref, oa_ref, ob_ref, wo_ref, gain_ref, wu_ref, wd_ref, y_ref, hid_ref):
    n_a = oa_ref.shape[1]
    x = (x_ref[...] + _dot(oa_ref[...].astype(BF16), wo_ref[0:n_a, :])
         + _dot(ob_ref[...].astype(BF16), wo_ref[n_a:, :]))
    y_ref[...] = _mlp(x, gain_ref, wu_ref, wd_ref, hid_ref)


def _mix_mlp(x, oa, ob, wo, gain, wu, wd, tm):
    t, d = x.shape
    row = lambda i: (i, 0)
    return pl.pallas_call(
        _mix_mlp_kernel,
        grid=(t // tm,),
        in_specs=[pl.BlockSpec((tm, d), row), pl.BlockSpec((tm, oa.shape[1]), row),
                  pl.BlockSpec((tm, ob.shape[1]), row), _const_spec(wo.shape),
                  _const_spec(gain.shape), _const_spec(wu.shape), _const_spec(wd.shape)],
        out_specs=pl.BlockSpec((tm, d), row),
        out_shape=jax.ShapeDtypeStruct((t, d), F32),
        scratch_shapes=[pltpu.VMEM((tm, wu.shape[1]), BF16)],
        compiler_params=_params("parallel"),
        name="mix_mlp",
    )(x, oa, ob, wo, gain, wu, wd)


def _glu_kernel(x_ref, gain_ref, w_ref, g_ref):
    h = _rms(x_ref[...], gain_ref[...]).astype(BF16)
    d = g_ref.shape[1]
    never = pl.program_id(0) < 0
    n_strips = 4
    sw = d // n_strips
    gated = []
    for c in range(n_strips):
        hc = h if c < 2 else _after(never, h, gated[c - 2])
        a = _dot(hc, w_ref[:, c * sw:(c + 1) * sw])
        b = _dot(hc, w_ref[:, d + c * sw:d + (c + 1) * sw])
        gated.append(a * _sigmoid(b))
        g_ref[:, c * sw:(c + 1) * sw] = gated[c]


def _glu(x, gain, w, tm):
    t, d = x.shape
    row = lambda i: (i, 0)
    return pl.pallas_call(
        _glu_kernel,
        grid=(t // tm,),
        in_specs=[pl.BlockSpec((tm, d), row), _const_spec(gain.shape), _const_spec(w.shape)],
        out_specs=pl.BlockSpec((tm, d), row),
        out_shape=jax.ShapeDtypeStruct((t, d), F32),
        compiler_params=_params("parallel"),
        name="glu",
    )(x, gain, w)


HALO = 32


def _convmod_mlp_kernel(x_ref, g_ref, prev_ref, buf_ref, dw_ref, dwb_ref, lng_ref, lnb_ref, wpw_ref,
                        gain_ref, wu_ref, wd_ref, y_ref, xp_ref, cv_ref, act_ref, hid_ref, *, nt, ntiles):
    step = pl.program_id(0)
    i = jnp.minimum(step, ntiles - 1) % nt
    bb, tm, d = g_ref.shape
    rows = bb * tm

    @pl.when(step == 0)
    def _():
        act_ref[...] = jnp.zeros_like(act_ref)

    x = x_ref[...].reshape(rows, d) + _dot(act_ref[...], wpw_ref[...])
    hn = _rms(x, gain_ref[...]).astype(BF16)

    xp_ref[:, 0:HALO, :] = jnp.where(i == 0, buf_ref[...], prev_ref[...])
    xp_ref[:, HALO:HALO + tm, :] = g_ref[...]
    n = HALO + tm
    first = HALO - (CONV_C - 1)
    n_chunks = wu_ref.shape[1] // 1024
    strip = d // (2 * n_chunks)
    after = functools.partial(_after, step < 0)

    def conv_strip(c):
        cs = c * strip
        done = []
        for bi in range(bb):
            xp = xp_ref[bi, :, cs:cs + strip]
            acc = None
            for r in range(8):
                xr = xp if r == 0 else pltpu.roll(xp, n - r, axis=0)
                for a in range((first + CONV_C - 1) // 8 + 1):
                    j = 8 * a + r - first
                    if 0 <= j < CONV_C:
                        term = xr[8 * a:8 * a + tm, :] * dw_ref[j:j + 1, cs:cs + strip]
                        acc = term if acc is None else acc + term
            cv_ref[bi * tm:(bi + 1) * tm, cs:cs + strip] = acc
            done.append(acc)
        return jnp.concatenate(done, axis=0)

    for c in range(n_chunks):
        cols = slice(c * 1024, (c + 1) * 1024)
        up = jnp.maximum(_dot(after(hn, conv_strip(c)), wu_ref[:, cols]), 0.0)
        hid_ref[:, cols] = (up * up).astype(BF16)
    y = x
    for c in range(n_chunks):
        cols = slice(c * 1024, (c + 1) * 1024)
        y = y + _dot(after(hid_ref[:, cols], conv_strip(n_chunks + c)), wd_ref[cols, :])

    cv = cv_ref[...] + dwb_ref[...]
    mu = jnp.mean(cv, axis=-1, keepdims=True)
    xc = cv - mu
    ln = xc * lax.rsqrt(jnp.mean(xc * xc, axis=-1, keepdims=True) + EPS) * lng_ref[...] + lnb_ref[...]
    act = (ln * _sigmoid(ln)).astype(BF16)
    act_ref[...] = act
    y_ref[...] = after(y, act).reshape(bb, tm, d)


def _convmod_mlp(x, g, buf, dw, dwb, lng, lnb, wpw, gain, wu, wd, tm, bb):
    b, l, d = x.shape
    nt = l // tm
    ntiles = (b // bb) * nt
    per = tm // HALO
    conv_tile = lambda s: jnp.minimum(s, ntiles - 1)
    mlp_tile = lambda s: jnp.maximum(s - 1, 0)
    cur = lambda s: (conv_tile(s) // nt, conv_tile(s) % nt, 0)
    prev = lambda s: (conv_tile(s) // nt, jnp.maximum((conv_tile(s) % nt) * per - 1, 0), 0)
    first = lambda s: (conv_tile(s) // nt, 0, 0)
    out = lambda s: (mlp_tile(s) // nt, mlp_tile(s) % nt, 0)
    return pl.pallas_call(
        functools.partial(_convmod_mlp_kernel, nt=nt, ntiles=ntiles),
        grid=(ntiles + 1,),
        in_specs=[pl.BlockSpec((bb, tm, d), out), pl.BlockSpec((bb, tm, d), cur),
                  pl.BlockSpec((bb, HALO, d), prev), pl.BlockSpec((bb, HALO, d), first),
                  _const_spec(dw.shape), _const_spec(dwb.shape), _const_spec(lng.shape),
                  _const_spec(lnb.shape), _const_spec(wpw.shape), _const_spec(gain.shape),
                  _const_spec(wu.shape), _const_spec(wd.shape)],
        out_specs=pl.BlockSpec((bb, tm, d), out),
        out_shape=jax.ShapeDtypeStruct((b, l, d), F32),
        scratch_shapes=[pltpu.VMEM((bb, HALO + tm, d), F32), pltpu.VMEM((bb * tm, d), F32),
                        pltpu.VMEM((bb * tm, d), BF16), pltpu.VMEM((bb * tm, wu.shape[1]), BF16)],
        compiler_params=_params("arbitrary"),
        name="convmod_mlp",
    )(x, g, g, buf, dw, dwb, lng, lnb, wpw, gain, wu, wd)


def _tile(n, pref):
    t = min(n, pref)
    assert n % t == 0
    return t


def _pad_lanes(v, offset=0):
    return jnp.zeros((1, LANES), F32).at[0, offset:offset + v.shape[0]].set(v.astype(F32))


def _trunk(x, past, w):
    b, l, d = x.shape
    t = b * l
    x2 = x.reshape(t, d)
    tm = _tile(t, 512)

    tq = _tile(l, 1024) if past is None else None
    qkv, z, kn, v, gates, *q_side = _inproj(x, w["norm_mix_e"], w["w_main"], w["a_log"], w["dt_bias"],
                                            w["b_f"], w["q_norm_b"], w["k_norm_b"], tm, tq)
    r3 = lambda a: a.reshape((b, a.shape[0] // b) + a.shape[1:])
    if past is None:
        buf_a = jnp.zeros((b, CONV_A - 1, qkv.shape[-1]), F32)
        s0 = jnp.zeros((b, H_A, DK_A, DV_A), F32)
    else:
        buf_a, s0 = past["conv_a"], past["delta"]
    o_a, nbuf_a, s_a = _gdn(r3(qkv), r3(z), r3(gates), buf_a, s0, w["conv_a_w"], w["g_norm_a"])

    if past is None:
        qa, ka, vt = q_side
        blocks = lambda a: a.reshape(b, H_B, l // tq, tq, QK_AUG)
        o_b = _fox(blocks(qa), blocks(ka), vt)
    else:
        qn, = q_side
        lfp = jnp.pad(jnp.transpose(past["lf_b"], (0, 2, 1)), ((0, 0), (0, 8 - H_B), (0, 0)))
        lf_new = r3(gates)[:, :, 2 * H_A:2 * H_A + H_B]
        lfn = jnp.pad(jnp.transpose(lf_new, (0, 2, 1)), ((0, 0), (0, 8 - H_B), (0, LANES - l)))
        flat = lambda c: c.reshape(b, c.shape[1] * H_B, D_B)
        o_b = _fox_dec(r3(qn), r3(kn), r3(v), flat(past["k_b"]), flat(past["v_b"]), lfp, lfn, r3(gates))

    x2 = _mix_mlp(x2, o_a.reshape(t, -1), o_b.reshape(t, -1), w["w_out"], w["norm_mlp0"],
                  w["w_up0"], w["w_down0"], tm)

    g = _glu(x2, w["norm_mix_o"], w["w_glu"], tm)
    g3 = g.reshape(b, l, d)
    if past is None:
        buf_c = jnp.zeros((b, CONV_C - 1, d), F32)
    else:
        buf_c = past["conv_c"]
    buf_pad = jnp.concatenate([jnp.zeros((b, HALO - (CONV_C - 1), d), F32), buf_c], axis=1)
    tc = _tile(l, 512)
    y = _convmod_mlp(x2.reshape(b, l, d), g3, buf_pad, w["dw_c"], w["dw_c_b"], w["ln_c_g"], w["ln_c_b"],
                     w["w_pw_c"], w["norm_mlp1"], w["w_up1"], w["w_down1"], tc, _tile(b, max(512 // tc, 1)))
    nbuf_c = jnp.concatenate([buf_c, g3], axis=1)[:, l:, :] if l < CONV_C - 1 else g3[:, l - (CONV_C - 1):, :]

    new = {
        "conv_a": nbuf_a[None], "delta": s_a[None],
        "k_b": kn.reshape(1, b, l, H_B, D_B), "v_b": v.reshape(1, b, l, H_B, D_B),
        "lf_b": r3(gates)[None, :, :, 2 * H_A:2 * H_A + H_B], "conv_c": nbuf_c[None],
    }
    return y, new


def kernel(x_prompt, x_sample, state_conv_a, state_delta_a, cache_k_b, cache_v_b, cache_logf_b, state_conv_c,
           norm_mix_e, w_in_e, b_f, conv_a_w, a_log, dt_bias, g_norm_a, q_norm_b, k_norm_b, w_out_e,
           norm_mix_o, w_glu, dw_c, dw_c_b, ln_c_g, ln_c_b, w_pw_c, norm_mlp, w_up, w_down):
    n_qkv = H_A * (2 * DK_A + DV_A)
    n_z = H_A * DV_A
    n_b = H_B * D_B
    w_in = w_in_e[0]
    o_a = n_qkv + n_z
    o_q = o_a + 2 * H_A
    o_f = o_q + 3 * n_b
    w_main = jnp.concatenate([w_in[:, :o_a], w_in[:, o_q:o_f], w_in[:, o_a:o_q], w_in[:, o_f:o_f + H_B],
                              jnp.zeros((w_in.shape[0], LANES - 2 * H_A - H_B), F32)], axis=1).astype(BF16)
    row = lambda a: a.reshape(1, -1).astype(F32)
    w = {
        "norm_mix_e": row(norm_mix_e[0]), "w_main": w_main,
        "a_log": _pad_lanes(a_log[0]), "dt_bias": _pad_lanes(dt_bias[0]),
        "b_f": _pad_lanes(b_f[0], 2 * H_A),
        "q_norm_b": row(q_norm_b[0]), "k_norm_b": row(k_norm_b[0]),
        "conv_a_w": conv_a_w[0], "g_norm_a": row(g_norm_a[0]),
        "w_out": w_out_e[0].astype(BF16),
        "norm_mlp0": row(norm_mlp[0]), "w_up0": w_up[0].astype(BF16), "w_down0": w_down[0].astype(BF16),
        "norm_mix_o": row(norm_mix_o[0]), "w_glu": w_glu[0].astype(BF16),
        "dw_c": dw_c[0], "dw_c_b": row(dw_c_b[0]), "ln_c_g": row(ln_c_g[0]), "ln_c_b": row(ln_c_b[0]),
        "w_pw_c": w_pw_c[0].astype(BF16),
        "norm_mlp1": row(norm_mlp[1]), "w_up1": w_up[1].astype(BF16), "w_down1": w_down[1].astype(BF16),
    }
    y_prompt, ps = _trunk(x_prompt, None, w)
    past = {"conv_a": state_conv_a[0], "delta": state_delta_a[0], "k_b": cache_k_b[0], "v_b": cache_v_b[0],
            "lf_b": cache_logf_b[0], "conv_c": state_conv_c[0]}
    y_sample, ss = _trunk(x_sample, past, w)
    return (y_prompt, y_sample,
            ps["conv_a"], ps["delta"], ps["k_b"], ps["v_b"], ps["lf_b"], ps["conv_c"],
            ss["conv_a"], ss["delta"], ss["k_b"], ss["v_b"], ss["lf_b"], ss["conv_c"])
```

```python
import functools
import math

import jax
import jax.numpy as jnp
from jax import lax
from jax.experimental import pallas as pl
from jax.experimental.pallas import tpu as pltpu

F32 = jnp.float32
BF16 = jnp.bfloat16
EPS = 1e-6
LANES = 128
VMEM_LIMIT = 56 * 1024 * 1024

H_A = 4
DK_A = 128
DV_A = 128
CONV_A = 4
H_B = 4
D_B = 128
CONV_C = 31
CHUNK = 64
QK_AUG = 2 * D_B
LOG2E = 1.4426950408889634


def _params(*sem, flags=None):
    return pltpu.CompilerParams(dimension_semantics=sem, vmem_limit_bytes=VMEM_LIMIT, flags=flags)


def _const_spec(shape):
    nd = len(shape)
    return pl.BlockSpec(shape, lambda *_: (0,) * nd, pipeline_mode=pl.Buffered(1))


def _rms(x, gain):
    return x * lax.rsqrt(jnp.mean(x * x, axis=-1, keepdims=True) + EPS) * gain


def _sigmoid(x):
    return 1.0 / (1.0 + jnp.exp(-x))


def _softplus(x):
    return jnp.maximum(x, 0.0) + jnp.log(1.0 + jnp.exp(-jnp.abs(x)))


def _dot(a, b):
    return jnp.dot(a, b, preferred_element_type=F32)


def _dot_nt(a, b):
    return lax.dot_general(a, b, (((1,), (1,)), ((), ())), preferred_element_type=F32)


def _dot_tn(a, b):
    return lax.dot_general(a, b, (((0,), (0,)), ((), ())), preferred_element_type=F32)


def _split3(x):
    hi = x.astype(BF16)
    r = x - hi.astype(F32)
    mid = r.astype(BF16)
    lo = (r - mid.astype(F32)).astype(BF16)
    return hi, mid, lo


def _after(never, value, witness):
    w = witness.shape[1]
    head = jnp.where(never, witness.astype(value.dtype), value[:, 0:w])
    return head if w == value.shape[1] else jnp.concatenate([head, value[:, w:]], axis=1)


def _cumsum_rows(x):
    n = x.shape[0]
    row = lax.broadcasted_iota(jnp.int32, x.shape, 0)
    s = 1
    while s < n:
        x = x + jnp.where(row >= s, pltpu.roll(x, s, axis=0), 0.0)
        s *= 2
    return x


def _cumsum_lanes(x):
    n = x.shape[1]
    col = lax.broadcasted_iota(jnp.int32, x.shape, 1)
    s = 1
    while s < n:
        x = x + jnp.where(col >= s, pltpu.roll(x, s, axis=1), 0.0)
        s *= 2
    return x


def _inproj_kernel(x_ref, gain_ref, wm_ref, alog_ref, dtb_ref, bf_ref, qg_ref, kg_ref, *refs, nt):
    if nt is None:
        qkv_ref, z_ref, kn_ref, v_ref, gates_ref, qn_ref = refs
    else:
        qkv_ref, z_ref, kn_ref, v_ref, gates_ref, qa_ref, ka_ref, vt_ref, carry = refs
    tm = x_ref.shape[0]
    h = _rms(x_ref[...], gain_ref[...]).astype(BF16)
    n_qkv = qkv_ref.shape[1]
    n_z = z_ref.shape[1]
    n_b = H_B * D_B
    off = n_qkv + n_z
    vg = _dot(h, wm_ref[:, off + 2 * n_b:off + 3 * n_b + LANES])
    q = _dot(h, wm_ref[:, off:off + n_b])
    k = _dot(h, wm_ref[:, off + n_b:off + 2 * n_b])
    g = vg[:, n_b:n_b + LANES]
    lane = lax.broadcasted_iota(jnp.int32, g.shape, 1)
    lg = -jnp.exp(alog_ref[...]) * _softplus(g + dtb_ref[...])
    beta = _sigmoid(g)
    lf = -_softplus(-(g + bf_ref[...]))
    gates = jnp.where(lane < H_A, lg, jnp.where(lane < 2 * H_A, beta,
                                                jnp.where(lane < 2 * H_A + H_B, lf, 0.0)))
    gates_ref[...] = gates
    if nt is not None:
        @pl.when(pl.program_id(0) % nt == 0)
        def _():
            carry[...] = jnp.zeros_like(carry)

        csum = _cumsum_rows(gates) + carry[...]
        carry[...] = csum[tm - 1:tm, :]
    never = pl.program_id(0) < 0
    piece = n_qkv // (H_B - 1)
    qkv_ref[:, 0:piece] = _dot(h, wm_ref[:, 0:piece])
    h_tied = h
    for hh in range(H_B):
        sl = slice(hh * D_B, (hh + 1) * D_B)
        qn = _rms(q[:, sl], qg_ref[...])
        kn = _rms(k[:, sl], kg_ref[...])
        heads_rows = pl.ds(hh, tm, stride=H_B)
        kn_ref[heads_rows, :] = kn
        v_ref[heads_rows, :] = vg[:, sl]
        if nt is None:
            qn_ref[:, sl] = qn
        else:
            c2 = jnp.broadcast_to(csum[:, 2 * H_A + hh:2 * H_A + hh + 1] * LOG2E, (tm, LANES))
            hi, mid, lo = (piece.astype(F32) for piece in _split3(c2))
            q_aug = jnp.where(lane == 0, hi, jnp.where(lane == 1, mid, jnp.where(lane == 2, lo,
                              jnp.where(lane < 6, 1.0, 0.0)))).astype(BF16)
            k_aug = jnp.where(lane < 3, 1.0, jnp.where(lane == 3, -hi, jnp.where(lane == 4, -mid,
                              jnp.where(lane == 5, -lo, 0.0)))).astype(BF16)
            q_main = (qn * (D_B ** -0.5 * LOG2E)).astype(BF16)
            k_main = kn.astype(BF16)
            qa_ref[0, hh, :, 0:D_B] = q_main
            qa_ref[0, hh, :, D_B:QK_AUG] = q_aug
            ka_ref[0, hh, :, 0:D_B] = k_main
            ka_ref[0, hh, :, D_B:QK_AUG] = k_aug
            vt_ref[0, hh, 0] = vg[:, sl].T.astype(BF16)
            for witness in (q_main, q_aug, k_main, k_aug):
                h_tied = _after(never, h_tied, witness)
        if hh < H_B - 2:
            cols = slice((hh + 1) * piece, (hh + 2) * piece)
            qkv_ref[:, cols] = _dot(h_tied, wm_ref[:, cols])
        elif hh == H_B - 2:
            z_ref[...] = _dot(h_tied, wm_ref[:, n_qkv:n_qkv + n_z])


def _inproj(x, gain, wm, alog, dtb, bfr, qg, kg, tm, fox_block=None):
    b, l, d = x.shape
    t = b * l
    n_qkv = H_A * (2 * DK_A + DV_A)
    n_z = H_A * DV_A
    n_b = H_B * D_B
    row = lambda i: (i, 0)
    outs = [(1, n_qkv), (1, n_z), (H_B, D_B), (H_B, D_B), (1, LANES)]
    out_specs = [pl.BlockSpec((tm * r, n), row) for r, n in outs]
    out_shape = [jax.ShapeDtypeStruct((t * r, n), F32) for r, n in outs]
    scratch = []
    if fox_block is None:
        nt = None
        out_specs.append(pl.BlockSpec((tm, n_b), row))
        out_shape.append(jax.ShapeDtypeStruct((t, n_b), F32))
    else:
        nt = l // tm
        per = fox_block // tm
        aug = pl.BlockSpec((1, H_B, tm, QK_AUG), lambda i: (i // nt, 0, i % nt, 0))
        out_specs += [aug, aug, pl.BlockSpec((1, H_B, 1, D_B, tm),
                                             lambda i: (i // nt, 0, (i % nt) // per, 0, (i % nt) % per))]
        out_shape += [jax.ShapeDtypeStruct((b, H_B, l, QK_AUG), BF16),
                      jax.ShapeDtypeStruct((b, H_B, l, QK_AUG), BF16),
                      jax.ShapeDtypeStruct((b, H_B, l // fox_block, D_B, fox_block), BF16)]
        scratch = [pltpu.VMEM((1, LANES), F32)]
    return pl.pallas_call(
        functools.partial(_inproj_kernel, nt=nt),
        grid=(t // tm,),
        in_specs=[pl.BlockSpec((tm, d), row), _const_spec(gain.shape), _const_spec(wm.shape),
                  _const_spec(alog.shape), _const_spec(dtb.shape),
                  _const_spec(bfr.shape), _const_spec(qg.shape), _const_spec(kg.shape)],
        out_specs=out_specs,
        out_shape=out_shape,
        scratch_shapes=scratch,
        compiler_params=_params("arbitrary"),
        name="inproj",
    )(x.reshape(t, d), gain, wm, alog, dtb, bfr, qg, kg)


def _each(f, *lists):
    return [f(*xs) for xs in zip(*lists)]


def _mm(x, y):
    return _dot(x.astype(BF16), y.astype(BF16))


def _unit_lower_inverse_minus_eye(a_list, same_block):
    d = _each(lambda a: jnp.where(same_block, a, 0.0), a_list)
    nn = _each(lambda a, dd: a - dd, a_list, d)
    d2 = _each(lambda x: _mm(x, x), d)
    d4 = _each(lambda x: _mm(x, x), d2)
    e = _each(lambda dd, x2: x2 - dd - _mm(dd, x2), d, d2)
    d8 = _each(lambda x: _mm(x, x), d4)
    e = _each(lambda ee, x4: ee + x4 + _mm(ee, x4), e, d4)
    e = _each(lambda ee, x8: ee + x8 + _mm(ee, x8), e, d8)
    m = _each(lambda ee, n: n + _mm(ee, n), e, nn)
    m2 = _each(lambda x: _mm(x, x), m)
    ex = _each(lambda ee, mm: ee - mm - _mm(mm, ee), e, m)
    return _each(lambda x, mm2: x + mm2 + _mm(mm2, x), ex, m2)


def _gdn_kernel(qkv_ref, z_ref, gates_ref, buf0_ref, s0_ref, cw_ref, gn_ref,
                o_ref, nbuf_ref, sout_ref, xbuf, state):
    c = pl.program_id(1)
    L = CHUNK
    bb, rows = qkv_ref.shape[0], qkv_ref.shape[1]
    nc = rows // L
    pad = 8
    bf = lambda x: x.astype(BF16)

    @pl.when(c == 0)
    def _():
        xbuf[:, pad - (CONV_A - 1):pad, :] = buf0_ref[...]
        state[...] = s0_ref[...]

    xbuf[:, pad:pad + rows, :] = qkv_ref[...]
    w = cw_ref[...]
    ys = []
    for bi in range(bb):
        y = xbuf[bi, pad:pad + rows, :] * w[CONV_A - 1:CONV_A, :]
        for i in range(1, CONV_A):
            y = y + xbuf[bi, pad - i:pad - i + rows, :] * w[CONV_A - 1 - i:CONV_A - i, :]
        ys.append(y * _sigmoid(y))
    tail = xbuf[:, pad + rows - (CONV_A - 1):pad + rows, :]
    xbuf[:, pad - (CONV_A - 1):pad, :] = tail
    nbuf_ref[...] = tail

    r = lax.broadcasted_iota(jnp.int32, (L, L), 0)
    cc = lax.broadcasted_iota(jnp.int32, (L, L), 1)
    causal = r >= cc
    strict = r > cc
    same_block = (r // 16) == (cc // 16)

    units = [(bi, ci, h) for bi in range(bb) for ci in range(nc) for h in range(H_A)]
    gam = {}
    for bi in range(bb):
        for ci in range(nc):
            gates = gates_ref[bi, ci * L:(ci + 1) * L, :]
            gam_all = _cumsum_rows(gates)
            gam_t = jnp.concatenate([gam_all, jnp.zeros_like(gam_all)], axis=0).T
            gam[bi, ci] = (gates, gam_all, gam_t)

    def normalized(bi, ci, h):
        rs = slice(ci * L, (ci + 1) * L)
        q = ys[bi][rs, h * DK_A:(h + 1) * DK_A]
        k = ys[bi][rs, H_A * DK_A + h * DK_A:H_A * DK_A + (h + 1) * DK_A]
        v = ys[bi][rs, 2 * H_A * DK_A + h * DV_A:2 * H_A * DK_A + (h + 1) * DV_A]
        q = q * lax.rsqrt(jnp.sum(q * q, axis=-1, keepdims=True) + EPS) * (DK_A ** -0.5)
        k = k * lax.rsqrt(jnp.sum(k * k, axis=-1, keepdims=True) + EPS)
        return q, k, v

    qkv_n = [normalized(*u) for u in units]
    q = [t[0] for t in qkv_n]
    k = [t[1] for t in qkv_n]
    v = [t[2] for t in qkv_n]
    gcol = [gam[bi, ci][1][:, h:h + 1] for bi, ci, h in units]
    grow = [gam[bi, ci][2][h:h + 1, 0:L] for bi, ci, h in units]
    beta = [gam[bi, ci][0][:, H_A + h:H_A + h + 1] for bi, ci, h in units]
    g_last = [gam[bi, ci][1][L - 1:L, h:h + 1] for bi, ci, h in units]
    decay = _each(lambda gc, gr: jnp.exp(jnp.where(causal, gc - gr, -jnp.inf)), gcol, grow)
    eg = _each(jnp.exp, gcol)
    kb = _each(bf, k)
    qb = _each(bf, q)
    kk = _each(_dot_nt, kb, kb)
    qk = _each(_dot_nt, qb, kb)
    a_mat = _each(lambda b_, d_, kk_: jnp.where(strict, b_ * d_ * kk_, 0.0), beta, decay, kk)
    p_mat = _each(lambda d_, qk_: bf(d_ * qk_), decay, qk)
    e_inv = _unit_lower_inverse_minus_eye(a_mat, same_block)
    x = _each(lambda b_, v_, eg_, k_: jnp.concatenate([b_ * v_, (b_ * eg_) * k_], axis=1), beta, v, eg, k)
    tx = _each(lambda x_, e_: x_ + _mm(e_, x_), x, e_inv)
    u0 = [t[:, 0:DV_A] for t in tx]
    wq = _each(lambda t, eg_, q_: jnp.concatenate([bf(t[:, DV_A:]), bf(eg_ * q_)], axis=0), tx, eg, q)
    kd = _each(lambda gl, gc, k_: bf(jnp.exp(gl - gc) * k_), g_last, gcol, k)
    dec = _each(jnp.exp, g_last)

    chains = [(bi, h) for bi in range(bb) for h in range(H_A)]
    s = [state[bi, h] for bi, h in chains]
    for ci in range(nc):
        ids = [units.index((bi, ci, h)) for bi, h in chains]
        ws_qs = [_dot(wq[u], bf(s_)) for u, s_ in zip(ids, s)]
        u_b = [bf(u0[u] - t[0:L]) for u, t in zip(ids, ws_qs)]
        pu = [_dot(p_mat[u], ub_) for u, ub_ in zip(ids, u_b)]
        ku = [_dot_tn(kd[u], ub_) for u, ub_ in zip(ids, u_b)]
        s = [dec[u] * s_ + ku_ for u, s_, ku_ in zip(ids, s, ku)]
        for (bi, h), t, pu_ in zip(chains, ws_qs, pu):
            zz = z_ref[bi, ci * L:(ci + 1) * L, h * DV_A:(h + 1) * DV_A]
            o_ref[bi, ci * L:(ci + 1) * L, h * DV_A:(h + 1) * DV_A] = (
                _rms(t[L:2 * L] + pu_, gn_ref[...]) * (zz * _sigmoid(zz)))
    for (bi, h), s_ in zip(chains, s):
        state[bi, h] = s_
        sout_ref[bi, h] = s_


GDN_CHUNKS_PER_STEP = 8


def _gdn(qkv, z, gates, buf0, s0, conv_w, g_norm):
    b, l, n_qkv = qkv.shape
    nc = min(l // CHUNK, 4)
    rows = nc * CHUNK
    bb = _tile(b, max(GDN_CHUNKS_PER_STEP // nc, 1))
    n_z = z.shape[2]
    blk = lambda bi, ci: (bi, ci, 0)
    per_b3 = lambda bi, ci: (bi, 0, 0)
    per_b4 = lambda bi, ci: (bi, 0, 0, 0)
    return pl.pallas_call(
        _gdn_kernel,
        grid=(b // bb, l // rows),
        in_specs=[pl.BlockSpec((bb, rows, n_qkv), blk), pl.BlockSpec((bb, rows, n_z), blk),
                  pl.BlockSpec((bb, rows, LANES), blk),
                  pl.BlockSpec((bb, CONV_A - 1, n_qkv), per_b3),
                  pl.BlockSpec((bb, H_A, DK_A, DV_A), per_b4),
                  _const_spec(conv_w.shape), _const_spec(g_norm.shape)],
        out_specs=[pl.BlockSpec((bb, rows, n_z), blk),
                   pl.BlockSpec((bb, CONV_A - 1, n_qkv), per_b3),
                   pl.BlockSpec((bb, H_A, DK_A, DV_A), per_b4)],
        out_shape=[jax.ShapeDtypeStruct((b, l, n_z), F32),
                   jax.ShapeDtypeStruct((b, CONV_A - 1, n_qkv), F32),
                   jax.ShapeDtypeStruct((b, H_A, DK_A, DV_A), F32)],
        scratch_shapes=[pltpu.VMEM((bb, 8 + rows, n_qkv), F32), pltpu.VMEM((bb, H_A, DK_A, DV_A), F32)],
        compiler_params=_params("parallel", "arbitrary"),
        name="gdn",
    )(qkv, z, gates, buf0, s0, conv_w, g_norm)


FOX_GROUP = 256


def _fox_kernel(qa_ref, ka_ref, vt_ref, o_ref, s_a, s_b, mx_a, mx_b, m_s, l_s, acc_s):
    i = pl.program_id(2)
    tq = qa_ref.shape[3]
    ng = tq // FOX_GROUP
    m_s[...] = jnp.full(m_s.shape, -jnp.inf, F32)
    l_s[...] = jnp.zeros(l_s.shape, F32)
    acc_s[...] = jnp.zeros(acc_s.shape, F32)

    def produce(j, s_ref, mx_ref):
        kj = ka_ref[0, 0, j]
        for g in range(ng):
            s = _dot_nt(kj, qa_ref[0, 0, 0, g * FOX_GROUP:(g + 1) * FOX_GROUP, :])
            s_ref[g] = s
            mx_ref[g] = jnp.max(s, axis=0, keepdims=True)

    def consume(j, s_ref, mx_ref, diagonal):
        for g in range(ng):
            rows = (g + 1) * FOX_GROUP if diagonal else s_ref.shape[1]
            s = s_ref[g, 0:rows, :]
            if diagonal:
                kpos = lax.broadcasted_iota(jnp.int32, s.shape, 0)
                qpos = lax.broadcasted_iota(jnp.int32, s.shape, 1) + g * FOX_GROUP
                s = jnp.where(kpos <= qpos, s, -jnp.inf)
            m_old = m_s[g]
            m_new = jnp.maximum(m_old, jnp.max(s, axis=0, keepdims=True) if diagonal else mx_ref[g])
            p = jnp.exp2(s - m_new)
            corr = jnp.exp2(m_old - m_new)
            l_s[g] = l_s[g] * corr + jnp.sum(p, axis=0, keepdims=True)
            acc_s[g] = acc_s[g] * corr + _dot(vt_ref[0, 0, j, :, 0:rows], p.astype(BF16))
            m_s[g] = m_new

    produce(0, s_a, mx_a)

    def pair(t, carry):
        j = 2 * t
        produce(j + 1, s_b, mx_b)
        consume(j, s_a, mx_a, False)
        produce(j + 2, s_a, mx_a)
        consume(j + 1, s_b, mx_b, False)
        return carry

    lax.fori_loop(0, i // 2, pair, 0)

    @pl.when(i % 2 == 0)
    def _():
        consume(i, s_a, mx_a, True)

    @pl.when(i % 2 == 1)
    def _():
        produce(i, s_b, mx_b)
        consume(i - 1, s_a, mx_a, False)
        consume(i, s_b, mx_b, True)

    for g in range(ng):
        o_ref[0, g * FOX_GROUP:(g + 1) * FOX_GROUP, :] = (acc_s[g] / l_s[g]).T


def _fox(qa, ka, vt):
    b, hb, nb, tq, _ = qa.shape
    ng = tq // FOX_GROUP
    return pl.pallas_call(
        _fox_kernel,
        grid=(b, hb, nb),
        in_specs=[pl.BlockSpec((1, 1, 1, tq, QK_AUG), lambda bi, h, i: (bi, h, i, 0, 0)),
                  pl.BlockSpec((1, 1, nb, tq, QK_AUG), lambda bi, h, i: (bi, h, 0, 0, 0)),
                  pl.BlockSpec((1, 1, nb, D_B, tq), lambda bi, h, i: (bi, h, 0, 0, 0))],
        out_specs=pl.BlockSpec((1, tq, D_B), lambda bi, h, i: (bi, i, h)),
        out_shape=jax.ShapeDtypeStruct((b, nb * tq, hb * D_B), F32),
        scratch_shapes=[pltpu.VMEM((ng, tq, FOX_GROUP), F32), pltpu.VMEM((ng, tq, FOX_GROUP), F32),
                        pltpu.VMEM((ng, 1, FOX_GROUP), F32), pltpu.VMEM((ng, 1, FOX_GROUP), F32),
                        pltpu.VMEM((ng, 1, FOX_GROUP), F32), pltpu.VMEM((ng, 1, FOX_GROUP), F32),
                        pltpu.VMEM((ng, D_B, FOX_GROUP), F32)],
        compiler_params=_params("parallel", "parallel", "arbitrary"),
        name="fox",
    )(qa, ka, vt)


def _fox_dec_kernel(q_ref, kn_ref, vn_ref, kp_ref, vp_ref, lfp_ref, lfn_ref, gates_ref, o_ref):
    L = q_ref.shape[1]
    t_past = lfp_ref.shape[2]
    bf = lambda x: x.astype(BF16)
    heads = range(H_B)
    c_past = _cumsum_lanes(lfp_ref[0])
    total = c_past[:, t_past - 1:t_past]
    c_new = total + _cumsum_lanes(lfn_ref[0])
    c_col = _cumsum_rows(gates_ref[0])
    c_q = [total[h:h + 1, :] + c_col[:, 2 * H_A + h:2 * H_A + h + 1] for h in heads]

    scale = D_B ** -0.5
    r = lax.broadcasted_iota(jnp.int32, (L, L), 0)
    cc = lax.broadcasted_iota(jnp.int32, (L, L), 1)
    hs = [slice(h * D_B, (h + 1) * D_B) for h in heads]
    q = [bf(q_ref[0, :, hs[h]]) for h in heads]
    past = [pl.ds(h, t_past, stride=H_B) for h in heads]
    s_past = [_dot_nt(q[h], bf(kp_ref[0, past[h], :])) for h in heads]
    new = [pl.ds(h, L, stride=H_B) for h in heads]
    s_new = [_dot_nt(q[h], bf(kn_ref[0, new[h], :])) for h in heads]
    s_past = [s_past[h] * scale + c_q[h] - c_past[h:h + 1, :] for h in heads]
    s_new = [jnp.where(cc <= r, s_new[h] * scale + c_q[h] - c_new[h:h + 1, 0:L], -jnp.inf) for h in heads]
    m = [jnp.maximum(jnp.max(s_past[h], axis=1, keepdims=True), jnp.max(s_new[h], axis=1, keepdims=True))
         for h in heads]
    p_past = [jnp.exp(s_past[h] - m[h]) for h in heads]
    p_new = [jnp.exp(s_new[h] - m[h]) for h in heads]
    denom = [jnp.sum(p_past[h], axis=1, keepdims=True) + jnp.sum(p_new[h], axis=1, keepdims=True) for h in heads]
    o = [_dot(bf(p_past[h]), bf(vp_ref[0, past[h], :])) + _dot(bf(p_new[h]), bf(vn_ref[0, new[h], :])) for h in heads]
    for h in heads:
        o_ref[0, :, hs[h]] = o[h] / denom[h]


def _fox_dec(qn, kn, v, k_past, v_past, lfp, lfn, gates):
    b, l, n_b = qn.shape
    rows_past = k_past.shape[1]
    t_past = rows_past // H_B
    per3 = lambda bi: (bi, 0, 0)
    return pl.pallas_call(
        _fox_dec_kernel,
        grid=(b,),
        in_specs=[pl.BlockSpec((1, l, n_b), per3), pl.BlockSpec((1, l * H_B, D_B), per3),
                  pl.BlockSpec((1, l * H_B, D_B), per3),
                  pl.BlockSpec((1, rows_past, D_B), per3), pl.BlockSpec((1, rows_past, D_B), per3),
                  pl.BlockSpec((1, 8, t_past), per3), pl.BlockSpec((1, 8, LANES), per3),
                  pl.BlockSpec((1, l, LANES), per3)],
        out_specs=pl.BlockSpec((1, l, n_b), per3),
        out_shape=jax.ShapeDtypeStruct((b, l, n_b), F32),
        compiler_params=_params("parallel"),
        name="fox_dec",
    )(qn, kn, v, k_past, v_past, lfp, lfn, gates)


def _mlp(x, gain_ref, wu_ref, wd_ref, hid_ref):
    hn = _rms(x, gain_ref[...]).astype(BF16)
    d_ff = wu_ref.shape[1]
    ck = 1024
    for c in range(d_ff // ck):
        a = jnp.maximum(_dot(hn, wu_ref[:, c * ck:(c + 1) * ck]), 0.0)
        hid_ref[:, c * ck:(c + 1) * ck] = (a * a).astype(BF16)
    return x + _dot(hid_ref[...], wd_ref[...])


def _mix_mlp_glu_kernel(x_ref, oa_ref, ob_ref, wo_ref, gain_ref, wu_ref, wd_ref, gain_o_ref, wg_ref,
                        y_ref, g_ref, hid_ref):
    n_a = oa_ref.shape[1]
    x = (x_ref[...] + _dot(oa_ref[...].astype(BF16), wo_ref[0:n_a, :])
         + _dot(ob_ref[...].astype(BF16), wo_ref[n_a:, :]))
    y = _mlp(x, gain_ref, wu_ref, wd_ref, hid_ref)
    y_ref[...] = y
    h = _rms(y, gain_o_ref[...]).astype(BF16)
    d = g_ref.shape[1]
    never = pl.program_id(0) < 0
    n_strips = 4
    sw = d // n_strips
    gated = []
    for c in range(n_strips):
        hc = h if c < 2 else _after(never, h, gated[c - 2])
        a = _dot(hc, wg_ref[:, c * sw:(c + 1) * sw])
        b = _dot(hc, wg_ref[:, d + c * sw:d + (c + 1) * sw])
        gated.append(a * _sigmoid(b))
        g_ref[:, c * sw:(c + 1) * sw] = gated[c]


def _mix_mlp_glu(x, oa, ob, wo, gain, wu, wd, gain_o, wg, tm):
    t, d = x.shape
    row = lambda i: (i, 0)
    return pl.pallas_call(
        _mix_mlp_glu_kernel,
        grid=(t // tm,),
        in_specs=[pl.BlockSpec((tm, d), row), pl.BlockSpec((tm, oa.shape[1]), row),
                  pl.BlockSpec((tm, ob.shape[1]), row), _const_spec(wo.shape),
                  _const_spec(gain.shape), _const_spec(wu.shape), _const_spec(wd.shape),
                  _const_spec(gain_o.shape), _const_spec(wg.shape)],
        out_specs=[pl.BlockSpec((tm, d), row), pl.BlockSpec((tm, d), row)],
        out_shape=[jax.ShapeDtypeStruct((t, d), F32), jax.ShapeDtypeStruct((t, d), F32)],
        scratch_shapes=[pltpu.VMEM((tm, wu.shape[1]), BF16)],
        compiler_params=_params("parallel"),
        name="mix_mlp_glu",
    )(x, oa, ob, wo, gain, wu, wd, gain_o, wg)


HALO = 32


def _convmod_mlp_kernel(x_ref, g_ref, prev_ref, buf_ref, dw_ref, dwb_ref, lng_ref, lnb_ref, wpw_ref,
                        gain_ref, wu_ref, wd_ref, y_ref, xp_ref, cv_ref, act_ref, hid_ref, *, nt, ntiles):
    step = pl.program_id(0)
    i = jnp.minimum(step, ntiles - 1) % nt
    bb, tm, d = g_ref.shape
    rows = bb * tm

    @pl.when(step == 0)
    def _():
        act_ref[...] = jnp.zeros_like(act_ref)

    x = x_ref[...].reshape(rows, d) + _dot(act_ref[...], wpw_ref[...])
    hn = _rms(x, gain_ref[...]).astype(BF16)

    xp_ref[:, 0:HALO, :] = jnp.where(i == 0, buf_ref[...], prev_ref[...])
    xp_ref[:, HALO:HALO + tm, :] = g_ref[...]
    n = HALO + tm
    first = HALO - (CONV_C - 1)
    n_chunks = wu_ref.shape[1] // 1024
    strip = d // (2 * n_chunks)
    after = functools.partial(_after, step < 0)

    def conv_strip(c):
        cs = c * strip
        done = []
        for bi in range(bb):
            xp = xp_ref[bi, :, cs:cs + strip]
            acc = None
            for r in range(8):
                xr = xp if r == 0 else pltpu.roll(xp, n - r, axis=0)
                for a in range((first + CONV_C - 1) // 8 + 1):
                    j = 8 * a + r - first
                    if 0 <= j < CONV_C:
                        term = xr[8 * a:8 * a + tm, :] * dw_ref[j:j + 1, cs:cs + strip]
                        acc = term if acc is None else acc + term
            cv_ref[bi * tm:(bi + 1) * tm, cs:cs + strip] = acc
            done.append(acc)
        return jnp.concatenate(done, axis=0)

    for c in range(n_chunks):
        cols = slice(c * 1024, (c + 1) * 1024)
        up = jnp.maximum(_dot(after(hn, conv_strip(c)), wu_ref[:, cols]), 0.0)
        hid_ref[:, cols] = (up * up).astype(BF16)
    y = x
    for c in range(n_chunks):
        cols = slice(c * 1024, (c + 1) * 1024)
        y = y + _dot(after(hid_ref[:, cols], conv_strip(n_chunks + c)), wd_ref[cols, :])

    cv = cv_ref[...] + dwb_ref[...]
    mu = jnp.mean(cv, axis=-1, keepdims=True)
    xc = cv - mu
    ln = xc * lax.rsqrt(jnp.mean(xc * xc, axis=-1, keepdims=True) + EPS) * lng_ref[...] + lnb_ref[...]
    act = (ln * _sigmoid(ln)).astype(BF16)
    act_ref[...] = act
    y_ref[...] = after(y, act).reshape(bb, tm, d)


def _convmod_mlp(x, g, buf, dw, dwb, lng, lnb, wpw, gain, wu, wd, tm, bb):
    b, l, d = x.shape
    nt = l // tm
    ntiles = (b // bb) * nt
    per = tm // HALO
    conv_tile = lambda s: jnp.minimum(s, ntiles - 1)
    mlp_tile = lambda s: jnp.maximum(s - 1, 0)
    cur = lambda s: (conv_tile(s) // nt, conv_tile(s) % nt, 0)
    prev = lambda s: (conv_tile(s) // nt, jnp.maximum((conv_tile(s) % nt) * per - 1, 0), 0)
    first = lambda s: (conv_tile(s) // nt, 0, 0)
    out = lambda s: (mlp_tile(s) // nt, mlp_tile(s) % nt, 0)
    return pl.pallas_call(
        functools.partial(_convmod_mlp_kernel, nt=nt, ntiles=ntiles),
        grid=(ntiles + 1,),
        in_specs=[pl.BlockSpec((bb, tm, d), out), pl.BlockSpec((bb, tm, d), cur),
                  pl.BlockSpec((bb, HALO, d), prev), pl.BlockSpec((bb, HALO, d), first),
                  _const_spec(dw.shape), _const_spec(dwb.shape), _const_spec(lng.shape),
                  _const_spec(lnb.shape), _const_spec(wpw.shape), _const_spec(gain.shape),
                  _const_spec(wu.shape), _const_spec(wd.shape)],
        out_specs=pl.BlockSpec((bb, tm, d), out),
        out_shape=jax.ShapeDtypeStruct((b, l, d), F32),
        scratch_shapes=[pltpu.VMEM((bb, HALO + tm, d), F32), pltpu.VMEM((bb * tm, d), F32),
                        pltpu.VMEM((bb * tm, d), BF16), pltpu.VMEM((bb * tm, wu.shape[1]), BF16)],
        compiler_params=_params("arbitrary"),
        name="convmod_mlp",
    )(x, g, g, buf, dw, dwb, lng, lnb, wpw, gain, wu, wd)


def _tile(n, pref):
    t = min(n, pref)
    assert n % t == 0
    return t


def _pad_lanes(v, offset=0):
    return jnp.zeros((1, LANES), F32).at[0, offset:offset + v.shape[0]].set(v.astype(F32))


def _trunk(x, past, w):
    b, l, d = x.shape
    t = b * l
    x2 = x.reshape(t, d)
    tm = _tile(t, 512)

    tq = _tile(l, 1024) if past is None else None
    qkv, z, kn, v, gates, *q_side = _inproj(x, w["norm_mix_e"], w["w_main"], w["a_log"], w["dt_bias"],
                                            w["b_f"], w["q_norm_b"], w["k_norm_b"], tm, tq)
    r3 = lambda a: a.reshape((b, a.shape[0] // b) + a.shape[1:])
    if past is None:
        buf_a = jnp.zeros((b, CONV_A - 1, qkv.shape[-1]), F32)
        s0 = jnp.zeros((b, H_A, DK_A, DV_A), F32)
    else:
        buf_a, s0 = past["conv_a"], past["delta"]
    o_a, nbuf_a, s_a = _gdn(r3(qkv), r3(z), r3(gates), buf_a, s0, w["conv_a_w"], w["g_norm_a"])

    if past is None:
        qa, ka, vt = q_side
        blocks = lambda a: a.reshape(b, H_B, l // tq, tq, QK_AUG)
        o_b = _fox(blocks(qa), blocks(ka), vt)
    else:
        qn, = q_side
        lfp = jnp.pad(jnp.transpose(past["lf_b"], (0, 2, 1)), ((0, 0), (0, 8 - H_B), (0, 0)))
        lf_new = r3(gates)[:, :, 2 * H_A:2 * H_A + H_B]
        lfn = jnp.pad(jnp.transpose(lf_new, (0, 2, 1)), ((0, 0), (0, 8 - H_B), (0, LANES - l)))
        flat = lambda c: c.reshape(b, c.shape[1] * H_B, D_B)
        o_b = _fox_dec(r3(qn), r3(kn), r3(v), flat(past["k_b"]), flat(past["v_b"]), lfp, lfn, r3(gates))

    x2, g = _mix_mlp_glu(x2, o_a.reshape(t, -1), o_b.reshape(t, -1), w["w_out"], w["norm_mlp0"],
                         w["w_up0"], w["w_down0"], w["norm_mix_o"], w["w_glu"], tm)
    g3 = g.reshape(b, l, d)
    if past is None:
        buf_c = jnp.zeros((b, CONV_C - 1, d), F32)
    else:
        buf_c = past["conv_c"]
    buf_pad = jnp.concatenate([jnp.zeros((b, HALO - (CONV_C - 1), d), F32), buf_c], axis=1)
    tc = _tile(l, 512)
    y = _convmod_mlp(x2.reshape(b, l, d), g3, buf_pad, w["dw_c"], w["dw_c_b"], w["ln_c_g"], w["ln_c_b"],
                     w["w_pw_c"], w["norm_mlp1"], w["w_up1"], w["w_down1"], tc, _tile(b, max(512 // tc, 1)))
    nbuf_c = jnp.concatenate([buf_c, g3], axis=1)[:, l:, :] if l < CONV_C - 1 else g3[:, l - (CONV_C - 1):, :]

    new = {
        "conv_a": nbuf_a[None], "delta": s_a[None],
        "k_b": kn.reshape(1, b, l, H_B, D_B), "v_b": v.reshape(1, b, l, H_B, D_B),
        "lf_b": r3(gates)[None, :, :, 2 * H_A:2 * H_A + H_B], "conv_c": nbuf_c[None],
    }
    return y, new


def kernel(x_prompt, x_sample, state_conv_a, state_delta_a, cache_k_b, cache_v_b, cache_logf_b, state_conv_c,
           norm_mix_e, w_in_e, b_f, conv_a_w, a_log, dt_bias, g_norm_a, q_norm_b, k_norm_b, w_out_e,
           norm_mix_o, w_glu, dw_c, dw_c_b, ln_c_g, ln_c_b, w_pw_c, norm_mlp, w_up, w_down):
    n_qkv = H_A * (2 * DK_A + DV_A)
    n_z = H_A * DV_A
    n_b = H_B * D_B
    w_in = w_in_e[0]
    o_a = n_qkv + n_z
    o_q = o_a + 2 * H_A
    o_f = o_q + 3 * n_b
    w_main = jnp.concatenate([w_in[:, :o_a], w_in[:, o_q:o_f], w_in[:, o_a:o_q], w_in[:, o_f:o_f + H_B],
                              jnp.zeros((w_in.shape[0], LANES - 2 * H_A - H_B), F32)], axis=1).astype(BF16)
    row = lambda a: a.reshape(1, -1).astype(F32)
    w = {
        "norm_mix_e": row(norm_mix_e[0]), "w_main": w_main,
        "a_log": _pad_lanes(a_log[0]), "dt_bias": _pad_lanes(dt_bias[0]),
        "b_f": _pad_lanes(b_f[0], 2 * H_A),
        "q_norm_b": row(q_norm_b[0]), "k_norm_b": row(k_norm_b[0]),
        "conv_a_w": conv_a_w[0], "g_norm_a": row(g_norm_a[0]),
        "w_out": w_out_e[0].astype(BF16),
        "norm_mlp0": row(norm_mlp[0]), "w_up0": w_up[0].astype(BF16), "w_down0": w_down[0].astype(BF16),
        "norm_mix_o": row(norm_mix_o[0]), "w_glu": w_glu[0].astype(BF16),
        "dw_c": dw_c[0], "dw_c_b": row(dw_c_b[0]), "ln_c_g": row(ln_c_g[0]), "ln_c_b": row(ln_c_b[0]),
        "w_pw_c": w_pw_c[0].astype(BF16),
        "norm_mlp1": row(norm_mlp[1]), "w_up1": w_up[1].astype(BF16), "w_down1": w_down[1].astype(BF16),
    }
    y_prompt, ps = _trunk(x_prompt, None, w)
    past = {"conv_a": state_conv_a[0], "delta": state_delta_a[0], "k_b": cache_k_b[0], "v_b": cache_v_b[0],
            "lf_b": cache_logf_b[0], "conv_c": state_conv_c[0]}
    y_sample, ss = _trunk(x_sample, past, w)
    return (y_prompt, y_sample,
            ps["conv_a"], ps["delta"], ps["k_b"], ps["v_b"], ps["lf_b"], ps["conv_c"],
            ss["conv_a"], ss["delta"], ss["k_b"], ss["v_b"], ss["lf_b"], ss["conv_c"])
```
